```python
import math
import jax, jax.numpy as jnp
from jax import lax
import numpy as np

D_MODEL = 1024
BATCH = 2
SEQ = 8192
DEPTH = 1

HEAD_DIM = 64
NSA_HEADS = (D_MODEL // 2) // HEAD_DIM
NSA_KV_HEADS = 2
NSA_REP = NSA_HEADS // NSA_KV_HEADS
CMP_BLOCK = 32
CMP_STRIDE = 16
SEL_BLOCK = 64
SEL_TOPK = 16
WINDOW = 512
DIFF_HEADS = (D_MODEL // 2) // (2 * HEAD_DIM)
DIFF_V_DIM = 2 * HEAD_DIM
DIFF_Q_BLOCK = 128
N_EXPERTS = 256
TOP_K = 8
N_GROUPS = 8
TOPK_GROUPS = 4
EXPERT_DIM = 256
SHARED_DIM = 256
ROUTED_SCALE = 2.5
EXPERT_CHUNK = 128
RMS_EPS = 1e-6
NEG_INF = -1e30
FORCE_BONUS = 1e4

NSA_Q_W = NSA_HEADS * HEAD_DIM
NSA_KV_W = NSA_KV_HEADS * HEAD_DIM
NSA_GATE_W = 3 * NSA_HEADS
DIFF_QK_W = DIFF_HEADS * 2 * HEAD_DIM
DIFF_V_W = DIFF_HEADS * DIFF_V_DIM
IN_SPLITS = (NSA_Q_W,) + (NSA_KV_W,) * 6 + (NSA_GATE_W, DIFF_QK_W, DIFF_QK_W, DIFF_V_W)
IN_WIDTH = sum(IN_SPLITS)
MIX_WIDTH = NSA_Q_W + DIFF_V_W

kernel_name = 'hybrid_nsa_diffattn_moe_block'


def rms_norm(x, g):
    xf = x.astype(jnp.float32)
    y = xf * lax.rsqrt(jnp.mean(xf * xf, axis=-1, keepdims=True) + RMS_EPS)
    return (y * g.astype(jnp.float32)).astype(x.dtype)


def alibi_slopes(n):
    return 2.0 ** (-8.0 * jnp.arange(1, n + 1, dtype=jnp.float32) / n)


def masked_softmax(s, mask):
    s = jnp.where(mask, s, NEG_INF)
    m = jnp.max(s, axis=-1, keepdims=True)
    e = jnp.where(mask, jnp.exp(s - m), 0.0)
    return e / jnp.maximum(jnp.sum(e, axis=-1, keepdims=True), 1e-30)


def compress_blocks(k, w1, w2, pe):
    b, s, g, dk = k.shape
    n_sub = CMP_BLOCK // CMP_STRIDE
    chunks = k.reshape(b, s // CMP_STRIDE, CMP_STRIDE, g, dk)
    n_cmp = s // CMP_STRIDE - n_sub + 1
    blocks = jnp.concatenate([chunks[:, i:i + n_cmp] for i in range(n_sub)], axis=2)
    blocks = blocks + pe[None, None, :, None, :]
    flat = jnp.transpose(blocks, (0, 1, 3, 2, 4)).reshape(b, n_cmp, g, CMP_BLOCK * dk)
    return jax.nn.gelu(flat @ w1) @ w2


def nsa_attention(q, k_cmp, v_cmp, k_sel, v_sel, k_win, v_win, gates, w_cmp1, w_cmp2, pe_cmp):
    b, s, h, dk = q.shape
    g, r = NSA_KV_HEADS, NSA_REP
    scale = dk ** -0.5
    slopes = alibi_slopes(NSA_HEADS).reshape(g, r)
    q = q.reshape(b, s, g, r, dk)
    gates = jax.nn.sigmoid(gates.astype(jnp.float32)).reshape(b, s, g, r, 3)
    kc = compress_blocks(k_cmp, w_cmp1[0], w_cmp2[0], pe_cmp[0])
    vc = compress_blocks(v_cmp, w_cmp1[1], w_cmp2[1], pe_cmp[1])
    n_cmp = kc.shape[1]
    cmp_start = jnp.arange(n_cmp) * CMP_STRIDE
    cmp_end = cmp_start + CMP_BLOCK - 1
    n_blk = s // SEL_BLOCK
    blk_start = jnp.arange(n_blk) * SEL_BLOCK
    overlap = jnp.clip(jnp.minimum(cmp_start[:, None] + CMP_BLOCK, blk_start[None, :] + SEL_BLOCK)
                       - jnp.maximum(cmp_start[:, None], blk_start[None, :]), 0, None).astype(jnp.float32) / CMP_BLOCK
    n_sel = min(SEL_TOPK, n_blk)
    ks_blk = k_sel.reshape(b, n_blk, SEL_BLOCK, g, dk).transpose(0, 3, 1, 2, 4)
    vs_blk = v_sel.reshape(b, n_blk, SEL_BLOCK, g, dk).transpose(0, 3, 1, 2, 4)
    kw_pad = jnp.pad(k_win, ((0, 0), (WINDOW, 0), (0, 0), (0, 0)))
    vw_pad = jnp.pad(v_win, ((0, 0), (WINDOW, 0), (0, 0), (0, 0)))
    b_idx = jnp.arange(b)[:, None, None, None]
    g_idx = jnp.arange(g)[None, :, None, None]
    in_blk = jnp.arange(SEL_BLOCK)
    win_len = WINDOW + SEL_BLOCK
    blk_ids = jnp.arange(n_blk)

    def query_block(qb):
        s0 = qb * SEL_BLOCK
        qt = lax.dynamic_slice_in_dim(q, s0, SEL_BLOCK, axis=1)
        t = s0 + in_blk
        sc = jnp.einsum('bqgrd,bngd->bgrqn', qt, kc).astype(jnp.float32) * scale
        sc = sc - slopes[:, :, None, None] * (t[:, None] - cmp_end[None, :]).astype(jnp.float32)
        p_c = masked_softmax(sc, cmp_end[None, :] <= t[:, None])
        o_c = jnp.einsum('bgrqn,bngd->bqgrd', p_c.astype(vc.dtype), vc)
        imp = jnp.einsum('bgrqn,nj->bgqj', p_c, overlap)
        forced = (blk_ids == 0) | (blk_ids == qb) | (blk_ids == qb - 1)
        imp = jnp.where(blk_ids <= qb, imp + jnp.where(forced, FORCE_BONUS, 0.0), NEG_INF)
        _, sel = lax.top_k(imp, n_sel)
        kg = ks_blk[b_idx, g_idx, sel].reshape(b, g, SEL_BLOCK, n_sel * SEL_BLOCK, dk)
        vg = vs_blk[b_idx, g_idx, sel].reshape(b, g, SEL_BLOCK, n_sel * SEL_BLOCK, dk)
        pos_s = (sel[..., None] * SEL_BLOCK + in_blk).reshape(b, g, SEL_BLOCK, n_sel * SEL_BLOCK)
        dist_s = (t[:, None] - pos_s)
        ss = jnp.einsum('bqgrd,bgqmd->bgrqm', qt, kg).astype(jnp.float32) * scale
        ss = ss - slopes[None, :, :, None, None] * dist_s[:, :, None].astype(jnp.float32)
        p_s = masked_softmax(ss, (dist_s >= 0)[:, :, None])
        o_s = jnp.einsum('bgrqm,bgqmd->bqgrd', p_s.astype(vg.dtype), vg)
        kw = lax.dynamic_slice_in_dim(kw_pad, s0, win_len, axis=1)
        vw = lax.dynamic_slice_in_dim(vw_pad, s0, win_len, axis=1)
        pos_w = s0 - WINDOW + jnp.arange(win_len)
        dist_w = t[:, None] - pos_w[None, :]
        mask_w = (dist_w >= 0) & (dist_w < WINDOW) & (pos_w[None, :] >= 0)
        sw = jnp.einsum('bqgrd,bkgd->bgrqk', qt, kw).astype(jnp.float32) * scale
        sw = sw - slopes[:, :, None, None] * dist_w.astype(jnp.float32)
        p_w = masked_softmax(sw, mask_w)
        o_w = jnp.einsum('bgrqk,bkgd->bqgrd', p_w.astype(vw.dtype), vw)
        gt = lax.dynamic_slice_in_dim(gates, s0, SEL_BLOCK, axis=1)
        o = gt[..., 0:1] * o_c + gt[..., 1:2] * o_s + gt[..., 2:3] * o_w
        return o.astype(q.dtype)

    out = lax.map(query_block, jnp.arange(s // SEL_BLOCK))
    return out.transpose(1, 0, 2, 3, 4, 5).reshape(b, s, h * dk)


def diff_attention(q, k, v, lam_params, subln_g, layer):
    b, s, h, _, dk = q.shape
    scale = dk ** -0.5
    lam_init = 0.8 - 0.6 * math.exp(-0.3 * layer)
    lp = lam_params.astype(jnp.float32)
    lam = jnp.exp(jnp.sum(lp[0] * lp[1])) - jnp.exp(jnp.sum(lp[2] * lp[3])) + lam_init
    slopes = alibi_slopes(h)
    kpos = jnp.arange(s)

    def query_block(qb):
        s0 = qb * DIFF_Q_BLOCK
        qt = lax.dynamic_slice_in_dim(q, s0, DIFF_Q_BLOCK, axis=1)
        t = s0 + jnp.arange(DIFF_Q_BLOCK)
        dist = t[:, None] - kpos[None, :]
        sc = jnp.einsum('bqhmd,bkhmd->bhmqk', qt, k).astype(jnp.float32) * scale
        sc = sc - slopes[:, None, None, None] * dist.astype(jnp.float32)
        p = masked_softmax(sc, dist >= 0)
        a = p[:, :, 0] - lam * p[:, :, 1]
        return jnp.einsum('bhqk,bkhd->bqhd', a.astype(v.dtype), v)

    o = lax.map(query_block, jnp.arange(s // DIFF_Q_BLOCK))
    o = o.transpose(1, 0, 2, 3, 4).reshape(b, s, h, DIFF_V_DIM)
    o = rms_norm(o, subln_g) * (1.0 - lam_init)
    return o.reshape(b, s, h * DIFF_V_DIM)


def swiglu(x, w_gu, w_down):
    a, gt = jnp.split(x @ w_gu, 2, axis=-1)
    return (jax.nn.silu(a) * gt) @ w_down


def moe_ffn(h, w_router, router_bias, w_gate_up, w_down, ws_gate_up, ws_down):
    b, s, d = h.shape
    n_tok = b * s
    x = h.reshape(n_tok, d)
    aff = jax.nn.sigmoid(jnp.dot(x.astype(jnp.float32), w_router.astype(jnp.float32)))
    choice = aff + router_bias.astype(jnp.float32)
    per_grp = N_EXPERTS // N_GROUPS
    grp_score = jnp.sum(lax.top_k(choice.reshape(n_tok, N_GROUPS, per_grp), 2)[0], axis=-1)
    _, grp_idx = lax.top_k(grp_score, TOPK_GROUPS)
    grp_mask = jnp.any(grp_idx[..., None] == jnp.arange(N_GROUPS), axis=-2)
    choice = jnp.where(jnp.repeat(grp_mask, per_grp, axis=-1), choice, NEG_INF)
    _, top_idx = lax.top_k(choice, TOP_K)
    top_w = jnp.take_along_axis(aff, top_idx, axis=-1)
    top_w = top_w / jnp.sum(top_w, axis=-1, keepdims=True) * ROUTED_SCALE
    n_asg = n_tok * TOP_K
    e_flat = top_idx.reshape(-1)
    tok_flat = jnp.arange(n_asg, dtype=jnp.int32) // TOP_K
    w_flat = top_w.reshape(-1)
    counts = jnp.bincount(e_flat, length=N_EXPERTS)
    padded = (counts + EXPERT_CHUNK - 1) // EXPERT_CHUNK * EXPERT_CHUNK
    start = jnp.cumsum(counts) - counts
    pend = jnp.cumsum(padded)
    pstart = pend - padded
    order = jnp.argsort(e_flat)
    e_sorted = e_flat[order]
    dest = pstart[e_sorted] + jnp.arange(n_asg) - start[e_sorted]
    n_chunks = (n_asg + N_EXPERTS * (EXPERT_CHUNK - 1) + EXPERT_CHUNK - 1) // EXPERT_CHUNK
    n_rows = n_chunks * EXPERT_CHUNK
    row_tok = jnp.full((n_rows,), n_tok, jnp.int32).at[dest].set(tok_flat[order])
    row_w = jnp.zeros((n_rows,), jnp.float32).at[dest].set(w_flat[order])
    chunk_e = jnp.minimum(jnp.searchsorted(pend, jnp.arange(n_chunks) * EXPERT_CHUNK, side='right'),
                          N_EXPERTS - 1)
    x_pad = jnp.concatenate([x, jnp.zeros((1, d), x.dtype)], axis=0)

    def expert_chunk(args):
        tok, wt, e = args
        y = swiglu(x_pad[tok], w_gate_up[e], w_down[e])
        return (y * wt[:, None]).astype(x.dtype)

    y_rows = lax.map(expert_chunk, (row_tok.reshape(n_chunks, EXPERT_CHUNK),
                                    row_w.reshape(n_chunks, EXPERT_CHUNK), chunk_e))
    routed = jax.ops.segment_sum(y_rows.reshape(n_rows, d), row_tok, num_segments=n_tok + 1)[:n_tok]
    shared = swiglu(x, ws_gate_up, ws_down)
    return (routed + shared).reshape(b, s, d)


def setup_inputs(seed: int = 0) -> dict:
    key = jax.random.key(seed)
    ks = jax.random.split(key, 18)

    def nrm(k, shape, scale):
        return jax.random.normal(k, shape, jnp.float32) * scale

    n_cmp_in = CMP_BLOCK * HEAD_DIM
    return {
        'x': nrm(ks[0], (BATCH, SEQ, D_MODEL), 1.0),
        'c': nrm(ks[1], (BATCH, D_MODEL), 1.0),
        'w_ada': nrm(ks[2], (DEPTH, D_MODEL, 6 * D_MODEL), 0.5 * D_MODEL ** -0.5),
        'b_ada': nrm(ks[3], (DEPTH, 6 * D_MODEL), 0.02),
        'norm_g': 1.0 + nrm(ks[4], (DEPTH, 4, D_MODEL), 0.05),
        'w_in': nrm(ks[5], (DEPTH, D_MODEL, IN_WIDTH), D_MODEL ** -0.5),
        'w_cmp1': nrm(ks[6], (DEPTH, 2, n_cmp_in, HEAD_DIM), n_cmp_in ** -0.5),
        'w_cmp2': nrm(ks[7], (DEPTH, 2, HEAD_DIM, HEAD_DIM), HEAD_DIM ** -0.5),
        'pe_cmp': nrm(ks[8], (DEPTH, 2, CMP_BLOCK, HEAD_DIM), 0.1),
        'diff_lambda': nrm(ks[9], (DEPTH, 4, HEAD_DIM), 0.1),
        'diff_subln': 1.0 + nrm(ks[10], (DEPTH, DIFF_V_DIM), 0.05),
        'w_out': nrm(ks[11], (DEPTH, MIX_WIDTH, D_MODEL), MIX_WIDTH ** -0.5),
        'w_router': nrm(ks[12], (DEPTH, D_MODEL, N_EXPERTS), D_MODEL ** -0.5),
        'router_bias': nrm(ks[13], (DEPTH, N_EXPERTS), 0.01),
        'w_gate_up': nrm(ks[14], (DEPTH, N_EXPERTS, D_MODEL, 2 * EXPERT_DIM), D_MODEL ** -0.5),
        'w_down': nrm(ks[15], (DEPTH, N_EXPERTS, EXPERT_DIM, D_MODEL), EXPERT_DIM ** -0.5),
        'ws_gate_up': nrm(ks[16], (DEPTH, D_MODEL, 2 * SHARED_DIM), D_MODEL ** -0.5),
        'ws_down': nrm(ks[17], (DEPTH, SHARED_DIM, D_MODEL), SHARED_DIM ** -0.5),
    }


def reference(x, c, w_ada, b_ada, norm_g, w_in, w_cmp1, w_cmp2, pe_cmp, diff_lambda, diff_subln,
              w_out, w_router, router_bias, w_gate_up, w_down, ws_gate_up, ws_down):
    b, s, d = x.shape
    split_at = np.cumsum(IN_SPLITS)[:-1].tolist()
    for layer in range(DEPTH):
        mod = jax.nn.silu(c) @ w_ada[layer] + b_ada[layer]
        sh1, sc1, g1, sh2, sc2, g2 = [m[:, None, :] for m in jnp.split(mod, 6, axis=-1)]
        hmix = rms_norm(x, norm_g[layer, 0]) * (1.0 + sc1) + sh1
        proj = hmix @ w_in[layer]
        nq, kc, vc, ksel, vsel, kwin, vwin, ngate, dq, dkk, dv = jnp.split(proj, split_at, axis=-1)
        kv_shape = (b, s, NSA_KV_HEADS, HEAD_DIM)
        o_nsa = nsa_attention(nq.reshape(b, s, NSA_HEADS, HEAD_DIM),
                              kc.reshape(kv_shape), vc.reshape(kv_shape),
                              ksel.reshape(kv_shape), vsel.reshape(kv_shape),
                              kwin.reshape(kv_shape), vwin.reshape(kv_shape),
                              ngate.reshape(b, s, NSA_HEADS, 3),
                              w_cmp1[layer], w_cmp2[layer], pe_cmp[layer])
        o_diff = diff_attention(dq.reshape(b, s, DIFF_HEADS, 2, HEAD_DIM),
                                dkk.reshape(b, s, DIFF_HEADS, 2, HEAD_DIM),
                                dv.reshape(b, s, DIFF_HEADS, DIFF_V_DIM),
                                diff_lambda[layer], diff_subln[layer], layer)
        y = jnp.concatenate([o_nsa, o_diff], axis=-1) @ w_out[layer]
        x = x + g1 * rms_norm(y, norm_g[layer, 1])
        hffn = rms_norm(x, norm_g[layer, 2]) * (1.0 + sc2) + sh2
        y = moe_ffn(hffn, w_router[layer], router_bias[layer], w_gate_up[layer], w_down[layer],
                    ws_gate_up[layer], ws_down[layer])
        x = x + g2 * rms_norm(y, norm_g[layer, 3])
    return x
```

```python
import functools
import math

import jax
import jax.numpy as jnp
from jax import lax
from jax.experimental import pallas as pl
from jax.experimental.pallas import tpu as pltpu

HEAD_DIM = 64
NSA_HEADS = 8
NSA_KV_HEADS = 2
NSA_REP = NSA_HEADS // NSA_KV_HEADS
CMP_BLOCK = 32
CMP_STRIDE = 16
SEL_BLOCK = 64
SEL_TOPK = 16
WINDOW = 512
DIFF_HEADS = 4
DIFF_V_DIM = 2 * HEAD_DIM
N_EXPERTS = 256
TOP_K = 8
N_GROUPS = 8
TOPK_GROUPS = 4
EXPERT_DIM = 256
SHARED_DIM = 256
ROUTED_SCALE = 2.5
EXPERT_CHUNK = 128
RMS_EPS = 1e-6
NEG_INF = -1e30
BELOW_NEG_INF = -3e38
FORCE_BONUS = 1e4

NSA_Q_W = NSA_HEADS * HEAD_DIM
NSA_KV_W = NSA_KV_HEADS * HEAD_DIM
NSA_GATE_W = 3 * NSA_HEADS
DIFF_QK_W = DIFF_HEADS * 2 * HEAD_DIM
DIFF_V_W = DIFF_HEADS * DIFF_V_DIM
IN_SPLITS = (NSA_Q_W,) + (NSA_KV_W,) * 6 + (NSA_GATE_W, DIFF_QK_W, DIFF_QK_W, DIFF_V_W)
IN_WIDTH = sum(IN_SPLITS)

LANES = 128
SUBLANES = 8
VMEM_LIMIT = 56 * 1024 * 1024

F32 = jnp.float32
BF16 = jnp.bfloat16


def _round_up(a, m):
    return (a + m - 1) // m * m


def _dot(a, b):
    return jnp.dot(a, b, preferred_element_type=F32)


def _dot_nt(a, b):
    return lax.dot_general(a, b, (((1,), (1,)), ((), ())), preferred_element_type=F32)


def _dot_split(a, b_bf16):
    hi = a.astype(BF16)
    lo = (a - hi.astype(F32)).astype(BF16)
    return _dot(hi, b_bf16) + _dot(lo, b_bf16)


def _dot_f32(a, b):
    a_hi = a.astype(BF16)
    a_lo = (a - a_hi.astype(F32)).astype(BF16)
    b_hi = b.astype(BF16)
    b_lo = (b - b_hi.astype(F32)).astype(BF16)
    return _dot(a_hi, b_hi) + (_dot(a_lo, b_hi) + _dot(a_hi, b_lo))


def _rms(x, g):
    return x * lax.rsqrt(jnp.mean(x * x, axis=-1, keepdims=True) + RMS_EPS) * g


def _params(*sem):
    return pltpu.CompilerParams(dimension_semantics=sem, vmem_limit_bytes=VMEM_LIMIT)


def _ada_kernel(c_ref, w_ref, b_ref, o_ref):
    c = c_ref[...]
    h = c * jax.nn.sigmoid(c)
    o_ref[...] = _dot_f32(h, w_ref[...]) + b_ref[...]


def _ada(c, w, b):
    bsz, d = c.shape
    n = w.shape[1]
    tn = n // 4
    cp = jnp.zeros((SUBLANES, d), F32).at[:bsz].set(c)
    out = pl.pallas_call(
        _ada_kernel,
        grid=(n // tn,),
        in_specs=[pl.BlockSpec((SUBLANES, d), lambda j: (0, 0)),
                  pl.BlockSpec((d, tn), lambda j: (0, j)),
                  pl.BlockSpec((1, tn), lambda j: (0, j))],
        out_specs=pl.BlockSpec((SUBLANES, tn), lambda j: (0, j)),
        out_shape=jax.ShapeDtypeStruct((SUBLANES, n), F32),
        compiler_params=_params("arbitrary"),
    )(cp, w, b.reshape(1, n))
    return out[:bsz]


def _inproj_kernel(x_ref, sc_ref, sh_ref, g_ref, w_ref, o_ref):
    h = _rms(x_ref[0], g_ref[...]) * (1.0 + sc_ref[0]) + sh_ref[0]
    o_ref[0] = _dot(h.astype(BF16), w_ref[...]).astype(o_ref.dtype)


def _inproj(x, sc, sh, g, w_bf16, tm):
    bsz, s, d = x.shape
    n = w_bf16.shape[1]
    return pl.pallas_call(
        _inproj_kernel,
        grid=(bsz, s // tm),
        in_specs=[pl.BlockSpec((1, tm, d), lambda b, i: (b, i, 0)),
                  pl.BlockSpec((1, 1, d), lambda b, i: (b, 0, 0)),
                  pl.BlockSpec((1, 1, d), lambda b, i: (b, 0, 0)),
                  pl.BlockSpec((1, d), lambda b, i: (0, 0)),
                  pl.BlockSpec((d, n), lambda b, i: (0, 0))],
        out_specs=pl.BlockSpec((1, tm, n), lambda b, i: (b, i, 0)),
        out_shape=jax.ShapeDtypeStruct((bsz, s, n), BF16),
        compiler_params=_params("parallel", "parallel"),
    )(x, sc, sh, g, w_bf16)


def _cmp_kernel(ch_ref, w1_ref, w2_ref, pe_ref, o_ref):
    ch = ch_ref[0, 0, 0]
    half = ch.shape[1]
    a = _dot(ch, w1_ref[0, 0])
    bm = _dot(ch, w1_ref[0, 1])
    n = bm.shape[0]
    bm_next = pltpu.roll(bm, shift=n - 1, axis=0)
    pe = pe_ref[0]
    pt = _dot(pe[:, :half], w1_ref[0, 0]) + _dot(pe[:, half:], w1_ref[0, 1])
    hmid = jax.nn.gelu(a + bm_next + pt[0:1])
    o_ref[0, 0, 0] = _dot(hmid.astype(BF16), w2_ref[0]).astype(o_ref.dtype)


def _compress(chunks, w1, w2, pe8):
    two, bsz, g, ncp, cw = chunks.shape
    dk = w2.shape[-1]
    return pl.pallas_call(
        _cmp_kernel,
        grid=(two, bsz, g),
        in_specs=[pl.BlockSpec((1, 1, 1, ncp, cw), lambda a, b, c: (a, b, c, 0, 0)),
                  pl.BlockSpec((1, 2, cw, dk), lambda a, b, c: (a, 0, 0, 0)),
                  pl.BlockSpec((1, dk, dk), lambda a, b, c: (a, 0, 0)),
                  pl.BlockSpec((1, SUBLANES, 2 * cw), lambda a, b, c: (a, 0, 0))],
        out_specs=pl.BlockSpec((1, 1, 1, ncp, dk), lambda a, b, c: (a, b, c, 0, 0)),
        out_shape=jax.ShapeDtypeStruct((two, bsz, g, ncp, dk), BF16),
        compiler_params=_params("parallel", "parallel", "parallel"),
    )(chunks, w1, w2, pe8)


def _nsa_kernel(slopes_ref, q_ref, kc_ref, vc_ref, ks_ref, vs_ref, kw_ref, vw_ref,
                g_ref, ov_ref, ex_ref, o_ref, *, tq):
    g = pl.program_id(1)
    i = pl.program_id(2)
    rep = NSA_REP
    s0 = i * tq
    q4 = q_ref[0].reshape(rep * tq, HEAD_DIM)
    t = s0 + lax.broadcasted_iota(jnp.int32, (tq, 1), 0)
    t4 = jnp.concatenate([t] * rep, axis=0)
    slope4 = jnp.concatenate(
        [jnp.full((tq, 1), slopes_ref[g * rep + r], F32) for r in range(rep)], axis=0)

    kc = kc_ref[0, 0]
    ncp = kc.shape[0]
    sc = _dot_nt(q4, kc)
    cend = lax.broadcasted_iota(jnp.int32, (1, ncp), 1) * CMP_STRIDE + (CMP_BLOCK - 1)
    dist = t4 - cend
    vis = dist >= 0
    sc = jnp.where(vis, sc - slope4 * dist.astype(F32), NEG_INF)
    mx = jnp.max(sc, axis=-1, keepdims=True)
    e = jnp.where(vis, jnp.exp(sc - mx), 0.0)
    p = e / jnp.maximum(jnp.sum(e, axis=-1, keepdims=True), 1e-30)
    o_c = _dot(p.astype(BF16), vc_ref[0, 0])

    psum = p[0:tq]
    for r in range(1, rep):
        psum = psum + p[r * tq:(r + 1) * tq]
    imp = _dot_split(psum, ov_ref[...])
    jf = lax.broadcasted_iota(jnp.int32, (tq, LANES), 1)
    qb = t // SEL_BLOCK
    forced = (jf == 0) | (jf == qb) | (jf == qb - 1)
    v = jnp.where(jf <= qb, imp + jnp.where(forced, FORCE_BONUS, 0.0), NEG_INF)
    jff = jf.astype(F32)
    selb = jnp.full((tq, LANES), NEG_INF, F32)
    for _ in range(SEL_TOPK):
        vmax = jnp.max(v, axis=-1, keepdims=True)
        first = jnp.min(jnp.where(v == vmax, jff, float(LANES)), axis=-1, keepdims=True)
        pick = jff == first
        selb = jnp.where(pick, 0.0, selb)
        v = jnp.where(pick, BELOW_NEG_INF, v)
    selb = selb.astype(BF16)

    def sel_step(kt, carry):
        m, l, acc = carry
        k0 = pl.multiple_of(kt * tq, tq)
        s = _dot_nt(q4, ks_ref[0, 0, pl.ds(k0, tq), :])
        pos = k0 + lax.broadcasted_iota(jnp.int32, (1, tq), 1)
        d = t4 - pos
        mb = _dot(selb, ex_ref[kt])
        s = s - slope4 * d.astype(F32) + jnp.concatenate([mb] * rep, axis=0)
        s = jnp.where(d >= 0, s, NEG_INF)
        m_new = jnp.maximum(m, jnp.max(s, axis=-1, keepdims=True))
        alpha = jnp.exp(m - m_new)
        pr = jnp.exp(s - m_new)
        l = alpha * l + jnp.sum(pr, axis=-1, keepdims=True)
        acc = alpha * acc + _dot(pr.astype(BF16), vs_ref[0, 0, pl.ds(k0, tq), :])
        return m_new, l, acc

    init = (jnp.full((rep * tq, 1), NEG_INF, F32), jnp.zeros((rep * tq, 1), F32),
            jnp.zeros((rep * tq, HEAD_DIM), F32))
    _, l_s, acc_s = lax.fori_loop(0, i + 1, sel_step, init)
    o_s = acc_s / l_s

    wl = WINDOW + tq
    ws = pl.multiple_of(jnp.maximum(s0 - WINDOW, 0), tq)
    sw = _dot_nt(q4, kw_ref[0, 0, pl.ds(ws, wl), :])
    dw = t4 - (ws + lax.broadcasted_iota(jnp.int32, (1, wl), 1))
    sw = jnp.where((dw >= 0) & (dw < WINDOW), sw - slope4 * dw.astype(F32), NEG_INF)
    ew = jnp.exp(sw - jnp.max(sw, axis=-1, keepdims=True))
    o_w = _dot(ew.astype(BF16), vw_ref[0, 0, pl.ds(ws, wl), :]) / jnp.sum(ew, axis=-1, keepdims=True)

    gate = jax.nn.sigmoid(g_ref[0, 0].astype(F32))
    for r in range(rep):
        rows = slice(r * tq, (r + 1) * tq)
        o = (gate[:, 3 * r:3 * r + 1] * o_c[rows] + gate[:, 3 * r + 1:3 * r + 2] * o_s[rows]
             + gate[:, 3 * r + 2:3 * r + 3] * o_w[rows])
        o_ref[0, r] = o.astype(o_ref.dtype)


def _nsa(slopes, q, kc, vc, ks, vs, kw, vw, gates, ov, ex, tq):
    bsz, h, s, dk = q.shape
    g = NSA_KV_HEADS
    rep = NSA_REP
    ncp = kc.shape[2]
    assert s >= WINDOW + tq and s % tq == 0 and tq % SEL_BLOCK == 0
    kv_spec = pl.BlockSpec((1, 1, s, dk), lambda b, c, i: (b, c, 0, 0))
    cmp_spec = pl.BlockSpec((1, 1, ncp, dk), lambda b, c, i: (b, c, 0, 0))
    return pl.pallas_call(
        functools.partial(_nsa_kernel, tq=tq),
        grid=(bsz, g, s // tq),
        in_specs=[pl.BlockSpec(memory_space=pltpu.SMEM),
                  pl.BlockSpec((1, rep, tq, dk), lambda b, c, i: (b, c, i, 0)),
                  cmp_spec, cmp_spec, kv_spec, kv_spec, kv_spec, kv_spec,
                  pl.BlockSpec((1, 1, tq, 3 * rep), lambda b, c, i: (b, c, i, 0)),
                  pl.BlockSpec(ov.shape, lambda b, c, i: (0, 0)),
                  pl.BlockSpec(ex.shape, lambda b, c, i: (0, 0, 0))],
        out_specs=pl.BlockSpec((1, rep, tq, dk), lambda b, c, i: (b, c, i, 0)),
        out_shape=jax.ShapeDtypeStruct((bsz, h, s, dk), BF16),
        compiler_params=_params("parallel", "parallel", "arbitrary"),
    )(slopes, q, kc, vc, ks, vs, kw, vw, gates, ov, ex)


def _diff_kernel(slopes_ref, q_ref, k_ref, v_ref, lam_ref, g_ref, o_ref, *, tq, lam_init):
    h = pl.program_id(1)
    i = pl.program_id(2)
    s0 = i * tq
    slope = slopes_ref[h]
    q0 = q_ref[0, 0, 0]
    q1 = q_ref[0, 0, 1]
    t = s0 + lax.broadcasted_iota(jnp.int32, (tq, 1), 0)

    def step(kt, carry):
        m, l, acc = carry
        k0 = pl.multiple_of(kt * tq, tq)
        sa = _dot_nt(q0, k_ref[0, 0, 0, pl.ds(k0, tq), :])
        sb = _dot_nt(q1, k_ref[0, 0, 1, pl.ds(k0, tq), :])
        d = t - (k0 + lax.broadcasted_iota(jnp.int32, (1, tq), 1))
        bias = jnp.where(d >= 0, -slope * d.astype(F32), NEG_INF)
        s = jnp.concatenate([sa + bias, sb + bias], axis=0)
        m_new = jnp.maximum(m, jnp.max(s, axis=-1, keepdims=True))
        alpha = jnp.exp(m - m_new)
        pr = jnp.exp(s - m_new)
        l = alpha * l + jnp.sum(pr, axis=-1, keepdims=True)
        acc = alpha * acc + _dot(pr.astype(BF16), v_ref[0, 0, pl.ds(k0, tq), :])
        return m_new, l, acc

    init = (jnp.full((2 * tq, 1), NEG_INF, F32), jnp.zeros((2 * tq, 1), F32),
            jnp.zeros((2 * tq, DIFF_V_DIM), F32))
    _, l, acc = lax.fori_loop(0, i + 1, step, init)
    o2 = acc / l
    lp = lam_ref[...]
    lam = (jnp.exp(jnp.sum(lp[0:1] * lp[1:2], axis=-1, keepdims=True))
           - jnp.exp(jnp.sum(lp[2:3] * lp[3:4], axis=-1, keepdims=True)) + lam_init)
    o = o2[0:tq] - lam * o2[tq:2 * tq]
    o_ref[0] = (_rms(o, g_ref[...]) * (1.0 - lam_init)).astype(o_ref.dtype)


def _diff(slopes, q, k, v, lam_params, subln_g, tq, lam_init):
    bsz, h, _, s, dk = q.shape
    dv = v.shape[-1]
    return pl.pallas_call(
        functools.partial(_diff_kernel, tq=tq, lam_init=lam_init),
        grid=(bsz, h, s // tq),
        in_specs=[pl.BlockSpec(memory_space=pltpu.SMEM),
                  pl.BlockSpec((1, 1, 2, tq, dk), lambda b, c, i: (b, c, 0, i, 0)),
                  pl.BlockSpec((1, 1, 2, s, dk), lambda b, c, i: (b, c, 0, 0, 0)),
                  pl.BlockSpec((1, 1, s, dv), lambda b, c, i: (b, c, 0, 0)),
                  pl.BlockSpec(lam_params.shape, lambda b, c, i: (0, 0)),
                  pl.BlockSpec((1, dv), lambda b, c, i: (0, 0))],
        out_specs=pl.BlockSpec((1, tq, dv), lambda b, c, i: (b, i, c)),
        out_shape=jax.ShapeDtypeStruct((bsz, s, h * dv), BF16),
        compiler_params=_params("parallel", "parallel", "arbitrary"),
    )(slopes, q, k, v, lam_params, subln_g.reshape(1, dv))


def _outproj_kernel(a_ref, b_ref, wo_ref, x_ref, g1_ref, sc_ref, sh_ref, ng1_ref, ng2_ref,
                    wr_ref, x1_ref, hb_ref, hrow_ref, lg_ref):
    half = a_ref.shape[-1]
    tm = a_ref.shape[1]
    y = _dot(a_ref[0], wo_ref[0:half, :]) + _dot(b_ref[0], wo_ref[half:, :])
    x1 = x_ref[0] + g1_ref[0] * _rms(y, ng1_ref[...])
    x1_ref[0] = x1
    h = _rms(x1, ng2_ref[...]) * (1.0 + sc_ref[0]) + sh_ref[0]
    hb_ref[0] = h.astype(BF16)
    for s in range(h.shape[1] // LANES):
        hrow_ref[pl.ds(s, tm, stride=SUBLANES), :] = h[:, s * LANES:(s + 1) * LANES]
    lg_ref[0] = _dot_f32(h, wr_ref[...])


def _outproj(a, b, wo, x, g1, sc2, sh2, ng1, ng2, wr, tm):
    bsz, s, d = x.shape
    half = a.shape[-1]
    ne = wr.shape[1]
    nt = s // tm
    vec = pl.BlockSpec((1, 1, d), lambda bb, i: (bb, 0, 0))
    row = pl.BlockSpec((1, d), lambda bb, i: (0, 0))
    tile = pl.BlockSpec((1, tm, d), lambda bb, i: (bb, i, 0))
    return pl.pallas_call(
        _outproj_kernel,
        grid=(bsz, nt),
        in_specs=[pl.BlockSpec((1, tm, half), lambda bb, i: (bb, i, 0)),
                  pl.BlockSpec((1, tm, half), lambda bb, i: (bb, i, 0)),
                  pl.BlockSpec((d, d), lambda bb, i: (0, 0)),
                  tile, vec, vec, vec, row, row,
                  pl.BlockSpec((d, ne), lambda bb, i: (0, 0))],
        out_specs=[tile, tile,
                   pl.BlockSpec((tm * d // LANES, LANES), lambda bb, i: (bb * nt + i, 0)),
                   pl.BlockSpec((1, tm, ne), lambda bb, i: (bb, i, 0))],
        out_shape=[jax.ShapeDtypeStruct((bsz, s, d), F32),
                   jax.ShapeDtypeStruct((bsz, s, d), BF16),
                   jax.ShapeDtypeStruct((bsz * s * d // LANES, LANES), F32),
                   jax.ShapeDtypeStruct((bsz, s, ne), F32)],
        compiler_params=_params("parallel", "parallel"),
    )(a, b, wo, x, g1, sc2, sh2, ng1, ng2, wr)


def _router_kernel(lg_ref, bias_ref, tri_ref, oi_ref, ow_ref, cnt_ref, carry_ref):
    @pl.when(pl.program_id(0) == 0)
    def _():
        carry_ref[...] = jnp.zeros_like(carry_ref)

    tm, ne = lg_ref.shape
    per_grp = ne // N_GROUPS
    aff = jax.nn.sigmoid(lg_ref[...])
    choice = aff + bias_ref[...]
    lane = lax.broadcasted_iota(jnp.int32, (tm, ne), 1)
    lanef = lane.astype(F32)
    grp = lane // per_grp

    def row_max(a):
        return jnp.max(a, axis=-1, keepdims=True)

    def first_lane(eq):
        return jnp.min(jnp.where(eq, lanef, float(ne)), axis=-1, keepdims=True)

    gscore = []
    for gi in range(N_GROUPS):
        mg = jnp.where(grp == gi, choice, BELOW_NEG_INF)
        m1 = row_max(mg)
        mg = jnp.where(lanef == first_lane(mg == m1), BELOW_NEG_INF, mg)
        gscore.append(m1 + row_max(mg))
    keep = jnp.zeros((tm, ne), F32)
    for gi in range(N_GROUPS):
        beaten = jnp.zeros((tm, 1), F32)
        for gj in range(N_GROUPS):
            if gj == gi:
                continue
            wins = gscore[gj] >= gscore[gi] if gj < gi else gscore[gj] > gscore[gi]
            beaten = beaten + jnp.where(wins, 1.0, 0.0)
        keep = jnp.where(grp == gi, jnp.where(beaten < TOPK_GROUPS, 1.0, 0.0), keep)
    ch = jnp.where(keep > 0.5, choice, NEG_INF)

    idx, wts = [], []
    onehot = jnp.zeros((tm, ne), F32)
    for _ in range(TOP_K):
        first = first_lane(ch == row_max(ch))
        pick = lanef == first
        wts.append(jnp.sum(jnp.where(pick, aff, 0.0), axis=-1, keepdims=True))
        ch = jnp.where(pick, BELOW_NEG_INF, ch)
        onehot = jnp.where(pick, 1.0, onehot)
        idx.append(first)
    wsum = wts[0]
    for w in wts[1:]:
        wsum = wsum + w
    before = carry_ref[...] + _dot(tri_ref[...], onehot.astype(BF16))
    carry_ref[...] = carry_ref[...] + jnp.sum(onehot, axis=0, keepdims=True)
    cnt_ref[...] = jnp.broadcast_to(carry_ref[...], cnt_ref.shape)

    l128 = lax.broadcasted_iota(jnp.int32, (tm, LANES), 1)
    oi = jnp.zeros((tm, LANES), jnp.int32)
    ow = jnp.zeros((tm, LANES), F32)
    for k in range(TOP_K):
        rank = jnp.sum(jnp.where(lanef == idx[k], before, 0.0), axis=-1, keepdims=True)
        oi = jnp.where(l128 == k, idx[k].astype(jnp.int32), oi)
        oi = jnp.where(l128 == TOP_K + k, rank.astype(jnp.int32), oi)
        ow = jnp.where(l128 == k, wts[k] / wsum * ROUTED_SCALE, ow)
    oi_ref[...] = oi
    ow_ref[...] = ow


def _router(logits, bias, tm):
    n_tok, ne = logits.shape
    tri = (jnp.arange(tm)[:, None] > jnp.arange(tm)[None, :]).astype(BF16)
    return pl.pallas_call(
        _router_kernel,
        grid=(n_tok // tm,),
        in_specs=[pl.BlockSpec((tm, ne), lambda i: (i, 0)),
                  pl.BlockSpec((1, ne), lambda i: (0, 0)),
                  pl.BlockSpec((tm, tm), lambda i: (0, 0))],
        out_specs=[pl.BlockSpec((tm, LANES), lambda i: (i, 0)),
                   pl.BlockSpec((tm, LANES), lambda i: (i, 0)),
                   pl.BlockSpec((SUBLANES, ne), lambda i: (0, 0))],
        out_shape=[jax.ShapeDtypeStruct((n_tok, LANES), jnp.int32),
                   jax.ShapeDtypeStruct((n_tok, LANES), F32),
                   jax.ShapeDtypeStruct((SUBLANES, ne), F32)],
        scratch_shapes=[pltpu.VMEM((1, ne), F32)],
        compiler_params=_params("arbitrary"),
    )(logits, bias.reshape(1, ne), tri)


def _row_copy(src, src_row, dst, dst_row, sem):
    return pltpu.make_async_copy(src.at[src_row], dst.at[dst_row], sem)


def _dispatch_kernel(idx_ref, rank_ref, pstart_ref, h_ref, init_ref, xs_ref, sem, *, th):
    del init_ref
    base = pl.program_id(0) * th

    def issue(tl, carry):
        for k in range(TOP_K):
            a = tl * TOP_K + k
            dst = pstart_ref[idx_ref[a]] + rank_ref[a]
            _row_copy(h_ref, base + tl, xs_ref, dst, sem).start()
        return carry

    lax.fori_loop(0, th, issue, 0)

    def drain(a, carry):
        _row_copy(h_ref, 0, xs_ref, 0, sem).wait()
        return carry

    lax.fori_loop(0, th * TOP_K, drain, 0)


def _dispatch(idx_flat, rank_flat, pstart, h_rows, n_rows, th):
    n_tok = h_rows.shape[0]
    init = jnp.zeros((n_rows,) + h_rows.shape[1:], h_rows.dtype)
    smem_blk = pl.BlockSpec((th * TOP_K,), lambda i: (i,), memory_space=pltpu.SMEM)
    return pl.pallas_call(
        functools.partial(_dispatch_kernel, th=th),
        grid=(n_tok // th,),
        in_specs=[smem_blk, smem_blk,
                  pl.BlockSpec(memory_space=pltpu.SMEM),
                  pl.BlockSpec(memory_space=pl.ANY),
                  pl.BlockSpec(memory_space=pl.ANY)],
        out_specs=pl.BlockSpec(memory_space=pl.ANY),
        out_shape=jax.ShapeDtypeStruct(init.shape, init.dtype),
        scratch_shapes=[pltpu.SemaphoreType.DMA(())],
        input_output_aliases={4: 0},
        compiler_params=_params("arbitrary"),
    )(idx_flat, rank_flat, pstart, h_rows, init)


def _experts_kernel(ce_ref, na_ref, x_ref, wgu_ref, wdn_ref, o_ref):
    i = pl.program_id(0)
    ch = EXPERT_CHUNK
    nsl = x_ref.shape[0] // ch

    @pl.when(i < na_ref[0])
    def _():
        x = jnp.concatenate(
            [x_ref[pl.ds(s, ch, stride=SUBLANES), :] for s in range(nsl)], axis=1).astype(BF16)
        au = _dot(x, wgu_ref[0].astype(BF16))
        a = au[:, :EXPERT_DIM]
        u = au[:, EXPERT_DIM:]
        hmid = (a * jax.nn.sigmoid(a) * u).astype(BF16)
        y = _dot(hmid, wdn_ref[0].astype(BF16))
        for s in range(nsl):
            o_ref[pl.ds(s, ch, stride=SUBLANES), :] = y[:, s * LANES:(s + 1) * LANES]

    @pl.when(i >= na_ref[0])
    def _():
        o_ref[...] = jnp.zeros_like(o_ref)


def _experts(chunk_e, n_active, xs2d, w_gu, w_dn):
    n_chunks = chunk_e.shape[0]
    _, d, two_e = w_gu.shape
    blk_rows = EXPERT_CHUNK * d // LANES

    def x_map(i, ce, na):
        return (jnp.minimum(i, na[0] - 1), 0)

    grid_spec = pltpu.PrefetchScalarGridSpec(
        num_scalar_prefetch=2,
        grid=(n_chunks,),
        in_specs=[pl.BlockSpec((blk_rows, LANES), x_map),
                  pl.BlockSpec((1, d, two_e), lambda i, ce, na: (ce[i], 0, 0)),
                  pl.BlockSpec((1, two_e // 2, d), lambda i, ce, na: (ce[i], 0, 0))],
        out_specs=pl.BlockSpec((blk_rows, LANES), lambda i, ce, na: (i, 0)),
    )
    return pl.pallas_call(
        _experts_kernel,
        grid_spec=grid_spec,
        out_shape=jax.ShapeDtypeStruct(xs2d.shape, F32),
        compiler_params=_params("arbitrary"),
    )(chunk_e, n_active, xs2d, w_gu, w_dn)


def _combine_kernel(idx_ref, rank_ref, w_ref, pstart_ref, y_ref, o_ref, buf, sem, *, tj):
    def issue(tl, carry):
        for k in range(TOP_K):
            a = tl * TOP_K + k
            src = pstart_ref[idx_ref[a]] + rank_ref[a]
            _row_copy(y_ref, src, buf, a, sem).start()
        return carry

    lax.fori_loop(0, tj, issue, 0)

    def drain(a, carry):
        _row_copy(y_ref, 0, buf, 0, sem).wait()
        return carry

    lax.fori_loop(0, tj * TOP_K, drain, 0)

    def reduce(tl, carry):
        acc = w_ref[tl * TOP_K] * buf[tl * TOP_K]
        for k in range(1, TOP_K):
            acc = acc + w_ref[tl * TOP_K + k] * buf[tl * TOP_K + k]
        o_ref[tl] = acc
        return carry

    lax.fori_loop(0, tj, reduce, 0)


def _combine(idx_flat, rank_flat, w_flat, pstart, y_rows, n_tok, tj):
    row_shape = y_rows.shape[1:]
    smem_blk = pl.BlockSpec((tj * TOP_K,), lambda i: (i,), memory_space=pltpu.SMEM)
    return pl.pallas_call(
        functools.partial(_combine_kernel, tj=tj),
        grid=(n_tok // tj,),
        in_specs=[smem_blk, smem_blk, smem_blk,
                  pl.BlockSpec(memory_space=pltpu.SMEM),
                  pl.BlockSpec(memory_space=pl.ANY)],
        out_specs=pl.BlockSpec((tj,) + row_shape, lambda i: (i, 0, 0)),
        out_shape=jax.ShapeDtypeStruct((n_tok,) + row_shape, F32),
        scratch_shapes=[pltpu.VMEM((tj * TOP_K,) + row_shape, F32),
                        pltpu.SemaphoreType.DMA(())],
        compiler_params=_params("arbitrary"),
    )(idx_flat, rank_flat, w_flat, pstart, y_rows)


def _final_kernel(hb_ref, r_ref, x1_ref, g2_ref, ng_ref, wgu_ref, wdn_ref, o_ref):
    tm, d = hb_ref.shape[1:]
    au = _dot(hb_ref[0], wgu_ref[...])
    a = au[:, :SHARED_DIM]
    u = au[:, SHARED_DIM:]
    shared = _dot((a * jax.nn.sigmoid(a) * u).astype(BF16), wdn_ref[...])
    routed = jnp.concatenate(
        [r_ref[pl.ds(s, tm, stride=SUBLANES), :] for s in range(d // LANES)], axis=1)
    o_ref[0] = x1_ref[0] + g2_ref[0] * _rms(routed + shared, ng_ref[...])


def _final(hb, routed2d, x1, g2, ng, wgu, wdn, tm):
    bsz, s, d = x1.shape
    nt = s // tm
    tile = pl.BlockSpec((1, tm, d), lambda bb, i: (bb, i, 0))
    return pl.pallas_call(
        _final_kernel,
        grid=(bsz, nt),
        in_specs=[tile,
                  pl.BlockSpec((tm * d // LANES, LANES), lambda bb, i: (bb * nt + i, 0)),
                  tile,
                  pl.BlockSpec((1, 1, d), lambda bb, i: (bb, 0, 0)),
                  pl.BlockSpec((1, d), lambda bb, i: (0, 0)),
                  pl.BlockSpec(wgu.shape, lambda bb, i: (0, 0)),
                  pl.BlockSpec(wdn.shape, lambda bb, i: (0, 0))],
        out_specs=tile,
        out_shape=jax.ShapeDtypeStruct((bsz, s, d), F32),
        compiler_params=_params("parallel", "parallel"),
    )(hb, routed2d, x1, g2, ng, wgu, wdn)


def _alibi_slopes(n):
    return (2.0 ** (-8.0 * jnp.arange(1, n + 1, dtype=F32) / n)).astype(F32)


def _layer(x, c, w_ada, b_ada, norm_g, w_in, w_cmp1, w_cmp2, pe_cmp, diff_lambda, diff_subln,
           w_out, w_router, router_bias, w_gate_up, w_down, ws_gate_up, ws_down, layer):
    bsz, s, d = x.shape
    n_tok = bsz * s
    tm = 256
    tq = 256
    dk = HEAD_DIM
    g = NSA_KV_HEADS

    mod = _ada(c, w_ada, b_ada)
    sh1, sc1, g1, sh2, sc2, g2 = [m[:, None, :] for m in jnp.split(mod, 6, axis=-1)]

    n_pad = _round_up(IN_WIDTH, LANES)
    w_in_p = jnp.zeros((d, n_pad), BF16).at[:, :IN_WIDTH].set(w_in.astype(BF16))
    proj = _inproj(x, sc1, sh1, norm_g[0:1], w_in_p, tm)

    offs = [0]
    for wdt in IN_SPLITS:
        offs.append(offs[-1] + wdt)
    pieces = [proj[..., offs[j]:offs[j + 1]] for j in range(len(IN_SPLITS))]
    nq, kcm, vcm, ksel, vsel, kwin, vwin, ngate, dq, dkk, dv = pieces
    scale = dk ** -0.5

    def heads_first(a, nh):
        return a.reshape(bsz, s, nh, dk).transpose(0, 2, 1, 3)

    q_nsa = heads_first(nq * scale, NSA_HEADS)
    ks, vs, kw, vw = [heads_first(a, g) for a in (ksel, vsel, kwin, vwin)]
    gates = ngate.reshape(bsz, s, g, 3 * NSA_REP).transpose(0, 2, 1, 3)

    n_sub = CMP_BLOCK // CMP_STRIDE
    assert n_sub == 2
    ncp = s // CMP_STRIDE
    chunks = jnp.stack([heads_first(kcm, g), heads_first(vcm, g)]).reshape(
        2, bsz, g, ncp, CMP_STRIDE * dk)
    w1 = w_cmp1.astype(BF16).reshape(2, n_sub, CMP_STRIDE * dk, dk)
    pe8 = jnp.broadcast_to(pe_cmp.astype(BF16).reshape(2, 1, CMP_BLOCK * dk),
                           (2, SUBLANES, CMP_BLOCK * dk))
    cmp_kv = _compress(chunks, w1, w_cmp2.astype(BF16), pe8)

    n_blk = s // SEL_BLOCK
    assert n_blk <= LANES and min(SEL_TOPK, n_blk) == SEL_TOPK
    cs = jnp.arange(ncp)[:, None] * CMP_STRIDE
    bs = jnp.arange(LANES)[None, :] * SEL_BLOCK
    ov = (jnp.clip(jnp.minimum(cs + CMP_BLOCK, bs + SEL_BLOCK) - jnp.maximum(cs, bs), 0, None)
          .astype(F32) / CMP_BLOCK).astype(BF16)
    key_blk = (jnp.arange(s) // SEL_BLOCK).reshape(s // tq, 1, tq)
    ex = (jnp.arange(LANES)[None, :, None] == key_blk).astype(BF16)

    o_nsa = _nsa(_alibi_slopes(NSA_HEADS), q_nsa, cmp_kv[0], cmp_kv[1], ks, vs, kw, vw,
                 gates, ov, ex, tq)
    o_nsa = o_nsa.transpose(0, 2, 1, 3).reshape(bsz, s, NSA_Q_W)

    def maps_first(a):
        return a.reshape(bsz, s, DIFF_HEADS, 2, dk).transpose(0, 2, 3, 1, 4)

    lam_init = 0.8 - 0.6 * math.exp(-0.3 * layer)
    o_diff = _diff(_alibi_slopes(DIFF_HEADS), maps_first(dq * scale), maps_first(dkk),
                   dv.reshape(bsz, s, DIFF_HEADS, DIFF_V_DIM).transpose(0, 2, 1, 3),
                   diff_lambda, diff_subln, tq, lam_init)

    x1, hb, h_rows2d, logits = _outproj(o_nsa, o_diff, w_out.astype(BF16), x, g1, sc2, sh2,
                                        norm_g[1:2], norm_g[2:3], w_router, tm)

    oi, ow, cnt = _router(logits.reshape(n_tok, N_EXPERTS), router_bias, tm)
    idx_flat = oi[:, :TOP_K].reshape(-1)
    rank_flat = oi[:, TOP_K:2 * TOP_K].reshape(-1)
    w_flat = ow[:, :TOP_K].reshape(-1)
    counts = cnt[0].astype(jnp.int32)
    padded = (counts + EXPERT_CHUNK - 1) // EXPERT_CHUNK * EXPERT_CHUNK
    pend = jnp.cumsum(padded)
    pstart = (pend - padded).astype(jnp.int32)
    n_asg = n_tok * TOP_K
    n_chunks = (n_asg + N_EXPERTS * (EXPERT_CHUNK - 1) + EXPERT_CHUNK - 1) // EXPERT_CHUNK
    n_rows = n_chunks * EXPERT_CHUNK
    chunk_e = jnp.minimum(
        jnp.searchsorted(pend, jnp.arange(n_chunks) * EXPERT_CHUNK, side='right'),
        N_EXPERTS - 1).astype(jnp.int32)
    n_active = (pend[-1:] // EXPERT_CHUNK).astype(jnp.int32)

    row_tile = (d // LANES, LANES)
    xs = _dispatch(idx_flat, rank_flat, pstart, h_rows2d.reshape((n_tok,) + row_tile), n_rows, 128)
    ys = _experts(chunk_e, n_active, xs.reshape(n_rows * d // LANES, LANES), w_gate_up, w_down)
    routed = _combine(idx_flat, rank_flat, w_flat, pstart, ys.reshape((n_rows,) + row_tile),
                      n_tok, 128)
    return _final(hb, routed.reshape(n_tok * d // LANES, LANES), x1, g2, norm_g[3:4],
                  ws_gate_up.astype(BF16), ws_down.astype(BF16), tm)


def kernel(x, c, w_ada, b_ada, norm_g, w_in, w_cmp1, w_cmp2, pe_cmp, diff_lambda, diff_subln,
           w_out, w_router, router_bias, w_gate_up, w_down, ws_gate_up, ws_down):
    for layer in range(w_ada.shape[0]):
        x = _layer(x, c, w_ada[layer], b_ada[layer], norm_g[layer], w_in[layer], w_cmp1[layer],
                   w_cmp2[layer], pe_cmp[layer], diff_lambda[layer], diff_subln[layer],
                   w_out[layer], w_router[layer], router_bias[layer], w_gate_up[layer],
                   w_down[layer], ws_gate_up[layer], ws_down[layer], layer)
    return x
```

```python
import functools
import math

import jax
import jax.numpy as jnp
from jax import lax
from jax.experimental import pallas as pl
from jax.experimental.pallas import tpu as pltpu

HEAD_DIM = 64
NSA_HEADS = 8
NSA_KV_HEADS = 2
NSA_REP = NSA_HEADS // NSA_KV_HEADS
CMP_BLOCK = 32
CMP_STRIDE = 16
SEL_BLOCK = 64
SEL_TOPK = 16
WINDOW = 512
DIFF_HEADS = 4
DIFF_V_DIM = 2 * HEAD_DIM
N_EXPERTS = 256
TOP_K = 8
N_GROUPS = 8
TOPK_GROUPS = 4
EXPERT_DIM = 256
SHARED_DIM = 256
ROUTED_SCALE = 2.5
EXPERT_CHUNK = 128
RMS_EPS = 1e-6
NEG_INF = -1e30
BELOW_NEG_INF = -3e38
FORCE_BONUS = 1e4

NSA_Q_W = NSA_HEADS * HEAD_DIM
NSA_KV_W = NSA_KV_HEADS * HEAD_DIM
NSA_GATE_W = 3 * NSA_HEADS
DIFF_QK_W = DIFF_HEADS * 2 * HEAD_DIM
DIFF_V_W = DIFF_HEADS * DIFF_V_DIM
IN_SPLITS = (NSA_Q_W,) + (NSA_KV_W,) * 6 + (NSA_GATE_W, DIFF_QK_W, DIFF_QK_W, DIFF_V_W)
IN_WIDTH = sum(IN_SPLITS)

LANES = 128
SUBLANES = 8
VMEM_LIMIT = 56 * 1024 * 1024

F32 = jnp.float32
BF16 = jnp.bfloat16


def _round_up(a, m):
    return (a + m - 1) // m * m


def _dot(a, b):
    return jnp.dot(a, b, preferred_element_type=F32)


def _dot_nt(a, b):
    return lax.dot_general(a, b, (((1,), (1,)), ((), ())), preferred_element_type=F32)


def _dot_split(a, b_bf16):
    hi = a.astype(BF16)
    lo = (a - hi.astype(F32)).astype(BF16)
    return _dot(hi, b_bf16) + _dot(lo, b_bf16)


def _dot_f32(a, b):
    a_hi = a.astype(BF16)
    a_lo = (a - a_hi.astype(F32)).astype(BF16)
    b_hi = b.astype(BF16)
    b_lo = (b - b_hi.astype(F32)).astype(BF16)
    return _dot(a_hi, b_hi) + (_dot(a_lo, b_hi) + _dot(a_hi, b_lo))


def _rms(x, g):
    return x * lax.rsqrt(jnp.mean(x * x, axis=-1, keepdims=True) + RMS_EPS) * g


def _params(*sem):
    return pltpu.CompilerParams(dimension_semantics=sem, vmem_limit_bytes=VMEM_LIMIT)


def _ada_kernel(c_ref, w_ref, b_ref, o_ref):
    c = c_ref[...]
    h = c * jax.nn.sigmoid(c)
    o_ref[...] = _dot_f32(h, w_ref[...]) + b_ref[...]


def _ada(c, w, b):
    bsz, d = c.shape
    n = w.shape[1]
    tn = n // 4
    cp = jnp.zeros((SUBLANES, d), F32).at[:bsz].set(c)
    out = pl.pallas_call(
        _ada_kernel,
        grid=(n // tn,),
        in_specs=[pl.BlockSpec((SUBLANES, d), lambda j: (0, 0)),
                  pl.BlockSpec((d, tn), lambda j: (0, j)),
                  pl.BlockSpec((1, tn), lambda j: (0, j))],
        out_specs=pl.BlockSpec((SUBLANES, tn), lambda j: (0, j)),
        out_shape=jax.ShapeDtypeStruct((SUBLANES, n), F32),
        compiler_params=_params("arbitrary"),
    )(cp, w, b.reshape(1, n))
    return out[:bsz]


def _inproj_kernel(x_ref, sc_ref, sh_ref, g_ref, w_ref, o_ref):
    h = _rms(x_ref[0], g_ref[...]) * (1.0 + sc_ref[0]) + sh_ref[0]
    o_ref[0] = _dot(h.astype(BF16), w_ref[...]).astype(o_ref.dtype)


def _inproj(x, sc, sh, g, w_bf16, tm):
    bsz, s, d = x.shape
    n = w_bf16.shape[1]
    return pl.pallas_call(
        _inproj_kernel,
        grid=(bsz, s // tm),
        in_specs=[pl.BlockSpec((1, tm, d), lambda b, i: (b, i, 0)),
                  pl.BlockSpec((1, 1, d), lambda b, i: (b, 0, 0)),
                  pl.BlockSpec((1, 1, d), lambda b, i: (b, 0, 0)),
                  pl.BlockSpec((1, d), lambda b, i: (0, 0)),
                  pl.BlockSpec((d, n), lambda b, i: (0, 0))],
        out_specs=pl.BlockSpec((1, tm, n), lambda b, i: (b, i, 0)),
        out_shape=jax.ShapeDtypeStruct((bsz, s, n), BF16),
        compiler_params=_params("parallel", "parallel"),
    )(x, sc, sh, g, w_bf16)


def _cmp_kernel(ch_ref, w1_ref, w2_ref, pe_ref, o_ref):
    ch = ch_ref[0, 0, 0]
    half = ch.shape[1]
    a = _dot(ch, w1_ref[0, 0])
    bm = _dot(ch, w1_ref[0, 1])
    n = bm.shape[0]
    bm_next = pltpu.roll(bm, shift=n - 1, axis=0)
    pe = pe_ref[0]
    pt = _dot(pe[:, :half], w1_ref[0, 0]) + _dot(pe[:, half:], w1_ref[0, 1])
    hmid = jax.nn.gelu(a + bm_next + pt[0:1])
    o_ref[0, 0, 0] = _dot(hmid.astype(BF16), w2_ref[0]).astype(o_ref.dtype)


def _compress(chunks, w1, w2, pe8):
    two, bsz, g, ncp, cw = chunks.shape
    dk = w2.shape[-1]
    return pl.pallas_call(
        _cmp_kernel,
        grid=(two, bsz, g),
        in_specs=[pl.BlockSpec((1, 1, 1, ncp, cw), lambda a, b, c: (a, b, c, 0, 0)),
                  pl.BlockSpec((1, 2, cw, dk), lambda a, b, c: (a, 0, 0, 0)),
                  pl.BlockSpec((1, dk, dk), lambda a, b, c: (a, 0, 0)),
                  pl.BlockSpec((1, SUBLANES, 2 * cw), lambda a, b, c: (a, 0, 0))],
        out_specs=pl.BlockSpec((1, 1, 1, ncp, dk), lambda a, b, c: (a, b, c, 0, 0)),
        out_shape=jax.ShapeDtypeStruct((two, bsz, g, ncp, dk), BF16),
        compiler_params=_params("parallel", "parallel", "parallel"),
    )(chunks, w1, w2, pe8)


AUG_W = LANES
AUG_HI = HEAD_DIM
AUG_LO = HEAD_DIM + 1
AUG_PAD = HEAD_DIM + 2
POS_SPLIT = 64


def _flash_tile(s, delta, carry, v):
    m, l, acc = carry
    m_new = jnp.maximum(m, jnp.max(s, axis=-1, keepdims=True) + delta)
    pr = jnp.exp(s - (m_new - delta))
    alpha = jnp.exp(m - m_new)
    return (m_new, alpha * l + jnp.sum(pr, axis=-1, keepdims=True),
            alpha * acc + _dot(pr.astype(BF16), v))


def _nsa_kernel(slopes_ref, q_ref, kc_ref, vc_ref, ks_ref, vs_ref, kw_ref, vw_ref,
                g_ref, ov_ref, o_ref, bw_ref, *, tq, tk):
    g = pl.program_id(1)
    i = pl.program_id(2)
    rep = NSA_REP
    rows = rep * tq
    s0 = i * tq
    qa = q_ref[0].reshape(rows, AUG_W)
    q4 = qa[:, :HEAD_DIM]
    t = s0 + lax.broadcasted_iota(jnp.int32, (tq, 1), 0)
    t4 = jnp.concatenate([t] * rep, axis=0)
    slope4 = jnp.concatenate(
        [jnp.full((tq, 1), slopes_ref[g * rep + r], F32) for r in range(rep)], axis=0)
    wl = WINDOW + tq

    @pl.when(i == 0)
    def _():
        row = lax.broadcasted_iota(jnp.int32, (tq, 1), 0)
        dw = (jnp.concatenate([row] * rep, axis=0)
              + (WINDOW - lax.broadcasted_iota(jnp.int32, (1, wl), 1)))
        bw_ref[...] = jnp.where((dw >= 0) & (dw < WINDOW), -slope4 * dw.astype(F32), NEG_INF)

    kc = kc_ref[0, 0]
    ncp = kc.shape[0]
    sc = _dot_nt(q4, kc)
    cend = lax.broadcasted_iota(jnp.int32, (1, ncp), 1) * CMP_STRIDE + (CMP_BLOCK - 1)
    dist = t4 - cend
    vis = dist >= 0
    sc = jnp.where(vis, sc - slope4 * dist.astype(F32), NEG_INF)
    mx = jnp.max(sc, axis=-1, keepdims=True)
    e = jnp.where(vis, jnp.exp(sc - mx), 0.0)
    p = e / jnp.maximum(jnp.sum(e, axis=-1, keepdims=True), 1e-30)
    o_c = _dot(p.astype(BF16), vc_ref[0, 0])

    psum = p[0:tq]
    for r in range(1, rep):
        psum = psum + p[r * tq:(r + 1) * tq]
    imp = _dot_split(psum, ov_ref[...])
    jf = lax.broadcasted_iota(jnp.int32, (tq, LANES), 1)
    qb = t // SEL_BLOCK
    forced = (jf == 0) | (jf == qb) | (jf == qb - 1)
    v = jnp.where(jf <= qb, imp + jnp.where(forced, FORCE_BONUS, 0.0), NEG_INF)
    jff = jf.astype(F32)
    selb = jnp.full((tq, LANES), NEG_INF, F32)
    for _ in range(SEL_TOPK):
        vmax = jnp.max(v, axis=-1, keepdims=True)
        first = jnp.min(jnp.where(v == vmax, jff, float(LANES)), axis=-1, keepdims=True)
        pick = jff == first
        selb = jnp.where(pick, 0.0, selb)
        v = jnp.where(pick, BELOW_NEG_INF, v)
    selb = selb.astype(BF16)

    qs = jnp.concatenate([qa, jnp.concatenate([selb] * rep, axis=0)], axis=1)

    def sel_tile(kt, carry, masked):
        k0 = pl.multiple_of(kt * tk, tk)
        s = _dot_nt(qs, ks_ref[0, 0, pl.ds(k0, tk), :])
        if masked:
            s = jnp.where(t4 >= k0 + lax.broadcasted_iota(jnp.int32, (1, tk), 1), s, NEG_INF)
        delta = slope4 * (k0 - s0).astype(F32)
        return _flash_tile(s, delta, carry, vs_ref[0, 0, pl.ds(k0, tk), :])

    init = (jnp.full((rows, 1), NEG_INF, F32), jnp.zeros((rows, 1), F32),
            jnp.zeros((rows, HEAD_DIM), F32))
    n_full = s0 // tk
    carry = lax.fori_loop(0, n_full, functools.partial(sel_tile, masked=False), init)
    _, l_s, acc_s = sel_tile(n_full, carry, True)
    o_s = acc_s / l_s

    w0 = pl.multiple_of(s0, tq)
    sw = _dot_nt(qa, kw_ref[0, 0, pl.ds(w0, wl), :]) + bw_ref[...]
    ew = jnp.exp(sw - jnp.max(sw, axis=-1, keepdims=True))
    o_w = _dot(ew.astype(BF16), vw_ref[0, 0, pl.ds(w0, wl), :]) / jnp.sum(ew, axis=-1, keepdims=True)

    gate = jax.nn.sigmoid(g_ref[0, 0].astype(F32))
    for r in range(rep):
        rr = slice(r * tq, (r + 1) * tq)
        o = (gate[:, 3 * r:3 * r + 1] * o_c[rr] + gate[:, 3 * r + 1:3 * r + 2] * o_s[rr]
             + gate[:, 3 * r + 2:3 * r + 3] * o_w[rr])
        o_ref[0, r] = o.astype(o_ref.dtype)


def _nsa(slopes, q, kc, vc, ks, vs, kw, vw, gates, ov, tq, tk):
    bsz, h, s, _ = q.shape
    g = NSA_KV_HEADS
    rep = NSA_REP
    dk = HEAD_DIM
    assert s % tk == 0 and tk % tq == 0 and tq % SEL_BLOCK == 0 and tk // POS_SPLIT <= 256

    def resident(a):
        return pl.BlockSpec((1, 1) + a.shape[2:], lambda b, c, i: (b, c, 0, 0))

    return pl.pallas_call(
        functools.partial(_nsa_kernel, tq=tq, tk=tk),
        grid=(bsz, g, s // tq),
        in_specs=[pl.BlockSpec(memory_space=pltpu.SMEM),
                  pl.BlockSpec((1, rep, tq, AUG_W), lambda b, c, i: (b, c, i, 0)),
                  resident(kc), resident(vc), resident(ks), resident(vs), resident(kw), resident(vw),
                  pl.BlockSpec((1, 1, tq, 3 * rep), lambda b, c, i: (b, c, i, 0)),
                  pl.BlockSpec(ov.shape, lambda b, c, i: (0, 0))],
        out_specs=pl.BlockSpec((1, rep, tq, dk), lambda b, c, i: (b, c, i, 0)),
        out_shape=jax.ShapeDtypeStruct((bsz, h, s, dk), BF16),
        scratch_shapes=[pltpu.VMEM((rep * tq, WINDOW + tq), F32)],
        compiler_params=_params("parallel", "parallel", "arbitrary"),
    )(slopes, q, kc, vc, ks, vs, kw, vw, gates, ov)


def _diff_kernel(slopes_ref, q_ref, k_ref, v_ref, lam_ref, g_ref, o_ref, *, tq, tk, lam_init):
    h = pl.program_id(1)
    i = pl.program_id(2)
    s0 = i * tq
    slope = slopes_ref[h]
    t = s0 + lax.broadcasted_iota(jnp.int32, (tq, 1), 0)

    def tile(kt, carry, masked):
        k0 = pl.multiple_of(kt * tk, tk)
        delta = slope * (k0 - s0).astype(F32)
        v = v_ref[0, 0, pl.ds(k0, tk), :]
        new = []
        for mi in range(2):
            s = _dot_nt(q_ref[0, 0, mi], k_ref[0, 0, mi, pl.ds(k0, tk), :])
            if masked:
                s = jnp.where(t >= k0 + lax.broadcasted_iota(jnp.int32, (1, tk), 1), s, NEG_INF)
            new.append(_flash_tile(s, delta, carry[mi], v))
        return tuple(new)

    one = (jnp.full((tq, 1), NEG_INF, F32), jnp.zeros((tq, 1), F32),
           jnp.zeros((tq, DIFF_V_DIM), F32))
    n_full = s0 // tk
    carry = lax.fori_loop(0, n_full, functools.partial(tile, masked=False), (one, one))
    (_, l0, acc0), (_, l1, acc1) = tile(n_full, carry, True)
    lp = lam_ref[...]
    lam = (jnp.exp(jnp.sum(lp[0:1] * lp[1:2], axis=-1, keepdims=True))
           - jnp.exp(jnp.sum(lp[2:3] * lp[3:4], axis=-1, keepdims=True)) + lam_init)
    o = acc0 / l0 - lam * (acc1 / l1)
    o_ref[0] = (_rms(o, g_ref[...]) * (1.0 - lam_init)).astype(o_ref.dtype)


def _diff(slopes, q, k, v, lam_params, subln_g, tq, tk, lam_init):
    bsz, h, _, s, _ = q.shape
    dv = v.shape[-1]
    assert s % tk == 0 and tk % tq == 0 and tk // POS_SPLIT <= 256
    return pl.pallas_call(
        functools.partial(_diff_kernel, tq=tq, tk=tk, lam_init=lam_init),
        grid=(bsz, h, s // tq),
        in_specs=[pl.BlockSpec(memory_space=pltpu.SMEM),
                  pl.BlockSpec((1, 1, 2, tq, AUG_W), lambda b, c, i: (b, c, 0, i, 0)),
                  pl.BlockSpec((1, 1, 2, s, AUG_W), lambda b, c, i: (b, c, 0, 0, 0)),
                  pl.BlockSpec((1, 1, s, dv), lambda b, c, i: (b, c, 0, 0)),
                  pl.BlockSpec(lam_params.shape, lambda b, c, i: (0, 0)),
                  pl.BlockSpec((1, dv), lambda b, c, i: (0, 0))],
        out_specs=pl.BlockSpec((1, tq, dv), lambda b, c, i: (b, i, c)),
        out_shape=jax.ShapeDtypeStruct((bsz, s, h * dv), BF16),
        compiler_params=_params("parallel", "parallel", "arbitrary"),
    )(slopes, q, k, v, lam_params, subln_g.reshape(1, dv))


def _outproj_kernel(a_ref, b_ref, wo_ref, x_ref, g1_ref, sc_ref, sh_ref, ng1_ref, ng2_ref,
                    wr_ref, x1_ref, hb_ref, hrow_ref, lg_ref):
    half = a_ref.shape[-1]
    tm = a_ref.shape[1]
    y = _dot(a_ref[0], wo_ref[0:half, :]) + _dot(b_ref[0], wo_ref[half:, :])
    x1 = x_ref[0] + g1_ref[0] * _rms(y, ng1_ref[...])
    x1_ref[0] = x1
    h = _rms(x1, ng2_ref[...]) * (1.0 + sc_ref[0]) + sh_ref[0]
    hb_ref[0] = h.astype(BF16)
    for s in range(h.shape[1] // LANES):
        hrow_ref[pl.ds(s, tm, stride=SUBLANES), :] = h[:, s * LANES:(s + 1) * LANES]
    lg_ref[0] = _dot_f32(h, wr_ref[...])


def _outproj(a, b, wo, x, g1, sc2, sh2, ng1, ng2, wr, tm):
    bsz, s, d = x.shape
    half = a.shape[-1]
    ne = wr.shape[1]
    nt = s // tm
    vec = pl.BlockSpec((1, 1, d), lambda bb, i: (bb, 0, 0))
    row = pl.BlockSpec((1, d), lambda bb, i: (0, 0))
    tile = pl.BlockSpec((1, tm, d), lambda bb, i: (bb, i, 0))
    return pl.pallas_call(
        _outproj_kernel,
        grid=(bsz, nt),
        in_specs=[pl.BlockSpec((1, tm, half), lambda bb, i: (bb, i, 0)),
                  pl.BlockSpec((1, tm, half), lambda bb, i: (bb, i, 0)),
                  pl.BlockSpec((d, d), lambda bb, i: (0, 0)),
                  tile, vec, vec, vec, row, row,
                  pl.BlockSpec((d, ne), lambda bb, i: (0, 0))],
        out_specs=[tile, tile,
                   pl.BlockSpec((tm * d // LANES, LANES), lambda bb, i: (bb * nt + i, 0)),
                   pl.BlockSpec((1, tm, ne), lambda bb, i: (bb, i, 0))],
        out_shape=[jax.ShapeDtypeStruct((bsz, s, d), F32),
                   jax.ShapeDtypeStruct((bsz, s, d), BF16),
                   jax.ShapeDtypeStruct((bsz * s * d // LANES, LANES), F32),
                   jax.ShapeDtypeStruct((bsz, s, ne), F32)],
        compiler_params=_params("parallel", "parallel"),
    )(a, b, wo, x, g1, sc2, sh2, ng1, ng2, wr)


def _router_kernel(lg_ref, bias_ref, tri_ref, oi_ref, ow_ref, cnt_ref, carry_ref):
    @pl.when(pl.program_id(0) == 0)
    def _():
        carry_ref[...] = jnp.zeros_like(carry_ref)

    tm, ne = lg_ref.shape
    per_grp = ne // N_GROUPS
    aff = jax.nn.sigmoid(lg_ref[...])
    choice = aff + bias_ref[...]
    lane = lax.broadcasted_iota(jnp.int32, (tm, ne), 1)
    lanef = lane.astype(F32)
    grp = lane // per_grp

    def row_max(a):
        return jnp.max(a, axis=-1, keepdims=True)

    def first_lane(eq):
        return jnp.min(jnp.where(eq, lanef, float(ne)), axis=-1, keepdims=True)

    gscore = []
    for gi in range(N_GROUPS):
        mg = jnp.where(grp == gi, choice, BELOW_NEG_INF)
        m1 = row_max(mg)
        mg = jnp.where(lanef == first_lane(mg == m1), BELOW_NEG_INF, mg)
        gscore.append(m1 + row_max(mg))
    keep = jnp.zeros((tm, ne), F32)
    for gi in range(N_GROUPS):
        beaten = jnp.zeros((tm, 1), F32)
        for gj in range(N_GROUPS):
            if gj == gi:
                continue
            wins = gscore[gj] >= gscore[gi] if gj < gi else gscore[gj] > gscore[gi]
            beaten = beaten + jnp.where(wins, 1.0, 0.0)
        keep = jnp.where(grp == gi, jnp.where(beaten < TOPK_GROUPS, 1.0, 0.0), keep)
    ch = jnp.where(keep > 0.5, choice, NEG_INF)

    idx, wts = [], []
    onehot = jnp.zeros((tm, ne), F32)
    for _ in range(TOP_K):
        first = first_lane(ch == row_max(ch))
        pick = lanef == first
        wts.append(jnp.sum(jnp.where(pick, aff, 0.0), axis=-1, keepdims=True))
        ch = jnp.where(pick, BELOW_NEG_INF, ch)
        onehot = jnp.where(pick, 1.0, onehot)
        idx.append(first)
    wsum = wts[0]
    for w in wts[1:]:
        wsum = wsum + w
    before = carry_ref[...] + _dot(tri_ref[...], onehot.astype(BF16))
    carry_ref[...] = carry_ref[...] + jnp.sum(onehot, axis=0, keepdims=True)
    cnt_ref[...] = jnp.broadcast_to(carry_ref[...], cnt_ref.shape)

    l128 = lax.broadcasted_iota(jnp.int32, (tm, LANES), 1)
    oi = jnp.zeros((tm, LANES), jnp.int32)
    ow = jnp.zeros((tm, LANES), F32)
    for k in range(TOP_K):
        rank = jnp.sum(jnp.where(lanef == idx[k], before, 0.0), axis=-1, keepdims=True)
        oi = jnp.where(l128 == k, idx[k].astype(jnp.int32), oi)
        oi = jnp.where(l128 == TOP_K + k, rank.astype(jnp.int32), oi)
        ow = jnp.where(l128 == k, wts[k] / wsum * ROUTED_SCALE, ow)
    oi_ref[...] = oi
    ow_ref[...] = ow


def _router(logits, bias, tm):
    n_tok, ne = logits.shape
    tri = (jnp.arange(tm)[:, None] > jnp.arange(tm)[None, :]).astype(BF16)
    return pl.pallas_call(
        _router_kernel,
        grid=(n_tok // tm,),
        in_specs=[pl.BlockSpec((tm, ne), lambda i: (i, 0)),
                  pl.BlockSpec((1, ne), lambda i: (0, 0)),
                  pl.BlockSpec((tm, tm), lambda i: (0, 0))],
        out_specs=[pl.BlockSpec((tm, LANES), lambda i: (i, 0)),
                   pl.BlockSpec((tm, LANES), lambda i: (i, 0)),
                   pl.BlockSpec((SUBLANES, ne), lambda i: (0, 0))],
        out_shape=[jax.ShapeDtypeStruct((n_tok, LANES), jnp.int32),
                   jax.ShapeDtypeStruct((n_tok, LANES), F32),
                   jax.ShapeDtypeStruct((SUBLANES, ne), F32)],
        scratch_shapes=[pltpu.VMEM((1, ne), F32)],
        compiler_params=_params("arbitrary"),
    )(logits, bias.reshape(1, ne), tri)


def _row_copy(src, src_row, dst, dst_row, sem):
    return pltpu.make_async_copy(src.at[src_row], dst.at[dst_row], sem)


def _chunk_fill(zbuf, xs_ref, start, sem):
    return pltpu.make_async_copy(zbuf, xs_ref.at[pl.ds(start, EXPERT_CHUNK)], sem)


def _dispatch_kernel(idx_ref, rank_ref, pstart_ref, pend_ref, h_ref, xs_ref, zbuf, sem, zsem, *, th):
    @pl.when(pl.program_id(0) == 0)
    def _():
        zbuf[...] = jnp.zeros_like(zbuf)

        def fill(e, carry):
            @pl.when(pend_ref[e] > pstart_ref[e])
            def _():
                _chunk_fill(zbuf, xs_ref, pend_ref[e] - EXPERT_CHUNK, zsem).start()
            return carry

        lax.fori_loop(0, N_EXPERTS, fill, 0)

        def fill_done(e, carry):
            @pl.when(pend_ref[e] > pstart_ref[e])
            def _():
                _chunk_fill(zbuf, xs_ref, 0, zsem).wait()
            return carry

        lax.fori_loop(0, N_EXPERTS, fill_done, 0)

        first_unused = pend_ref[N_EXPERTS - 1] // EXPERT_CHUNK
        n_chunks = xs_ref.shape[0] // EXPERT_CHUNK

        def tail(ci, carry):
            _chunk_fill(zbuf, xs_ref, ci * EXPERT_CHUNK, zsem).start()
            return carry

        lax.fori_loop(first_unused, n_chunks, tail, 0)

        def tail_done(ci, carry):
            _chunk_fill(zbuf, xs_ref, 0, zsem).wait()
            return carry

        lax.fori_loop(first_unused, n_chunks, tail_done, 0)

    def issue(tl, carry):
        for k in range(TOP_K):
            a = tl * TOP_K + k
            dst = pstart_ref[idx_ref[a]] + rank_ref[a]
            _row_copy(h_ref, tl, xs_ref, dst, sem).start()
        return carry

    lax.fori_loop(0, th, issue, 0)

    def drain(a, carry):
        _row_copy(h_ref, 0, xs_ref, 0, sem).wait()
        return carry

    lax.fori_loop(0, th * TOP_K, drain, 0)


def _dispatch(idx_flat, rank_flat, pstart, pend, h_rows, n_rows, th):
    n_tok = h_rows.shape[0]
    row_shape = h_rows.shape[1:]
    smem_blk = pl.BlockSpec((th * TOP_K,), lambda i: (i,), memory_space=pltpu.SMEM)
    return pl.pallas_call(
        functools.partial(_dispatch_kernel, th=th),
        grid=(n_tok // th,),
        in_specs=[smem_blk, smem_blk,
                  pl.BlockSpec(memory_space=pltpu.SMEM),
                  pl.BlockSpec(memory_space=pltpu.SMEM),
                  pl.BlockSpec((th,) + row_shape, lambda i: (i, 0, 0))],
        out_specs=pl.BlockSpec(memory_space=pl.ANY),
        out_shape=jax.ShapeDtypeStruct((n_rows,) + row_shape, h_rows.dtype),
        scratch_shapes=[pltpu.VMEM((EXPERT_CHUNK,) + row_shape, h_rows.dtype),
                        pltpu.SemaphoreType.DMA(()), pltpu.SemaphoreType.DMA(())],
        compiler_params=_params("arbitrary"),
    )(idx_flat, rank_flat, pstart, pend, h_rows)


def _experts_kernel(ce_ref, na_ref, x_ref, wgu_ref, wdn_ref, o_ref):
    i = pl.program_id(0)
    ch = EXPERT_CHUNK
    nsl = x_ref.shape[0] // ch

    @pl.when(i < na_ref[0])
    def _():
        x = jnp.concatenate(
            [x_ref[pl.ds(s, ch, stride=SUBLANES), :] for s in range(nsl)], axis=1).astype(BF16)
        au = _dot(x, wgu_ref[0].astype(BF16))
        a = au[:, :EXPERT_DIM]
        u = au[:, EXPERT_DIM:]
        hmid = (a * jax.nn.sigmoid(a) * u).astype(BF16)
        y = _dot(hmid, wdn_ref[0].astype(BF16))
        for s in range(nsl):
            o_ref[pl.ds(s, ch, stride=SUBLANES), :] = y[:, s * LANES:(s + 1) * LANES]

    @pl.when(i >= na_ref[0])
    def _():
        o_ref[...] = jnp.zeros_like(o_ref)


def _experts(chunk_e, n_active, xs2d, w_gu, w_dn):
    n_chunks = chunk_e.shape[0]
    _, d, two_e = w_gu.shape
    blk_rows = EXPERT_CHUNK * d // LANES

    def x_map(i, ce, na):
        return (jnp.minimum(i, na[0] - 1), 0)

    grid_spec = pltpu.PrefetchScalarGridSpec(
        num_scalar_prefetch=2,
        grid=(n_chunks,),
        in_specs=[pl.BlockSpec((blk_rows, LANES), x_map),
                  pl.BlockSpec((1, d, two_e), lambda i, ce, na: (ce[i], 0, 0)),
                  pl.BlockSpec((1, two_e // 2, d), lambda i, ce, na: (ce[i], 0, 0))],
        out_specs=pl.BlockSpec((blk_rows, LANES), lambda i, ce, na: (i, 0)),
    )
    return pl.pallas_call(
        _experts_kernel,
        grid_spec=grid_spec,
        out_shape=jax.ShapeDtypeStruct(xs2d.shape, F32),
        compiler_params=_params("arbitrary"),
    )(chunk_e, n_active, xs2d, w_gu, w_dn)


def _combine_kernel(idx_ref, rank_ref, w_ref, pstart_ref, y_ref, o_ref, buf, sem, *, tj):
    def issue(tl, carry):
        for k in range(TOP_K):
            a = tl * TOP_K + k
            src = pstart_ref[idx_ref[a]] + rank_ref[a]
            _row_copy(y_ref, src, buf, a, sem).start()
        return carry

    lax.fori_loop(0, tj, issue, 0)

    def drain(a, carry):
        _row_copy(y_ref, 0, buf, 0, sem).wait()
        return carry

    lax.fori_loop(0, tj * TOP_K, drain, 0)

    def reduce(tl, carry):
        acc = w_ref[tl * TOP_K] * buf[tl * TOP_K]
        for k in range(1, TOP_K):
            acc = acc + w_ref[tl * TOP_K + k] * buf[tl * TOP_K + k]
        o_ref[tl] = acc
        return carry

    lax.fori_loop(0, tj, reduce, 0)


def _combine(idx_flat, rank_flat, w_flat, pstart, y_rows, n_tok, tj):
    row_shape = y_rows.shape[1:]
    smem_blk = pl.BlockSpec((tj * TOP_K,), lambda i: (i,), memory_space=pltpu.SMEM)
    return pl.pallas_call(
        functools.partial(_combine_kernel, tj=tj),
        grid=(n_tok // tj,),
        in_specs=[smem_blk, smem_blk, smem_blk,
                  pl.BlockSpec(memory_space=pltpu.SMEM),
                  pl.BlockSpec(memory_space=pl.ANY)],
        out_specs=pl.BlockSpec((tj,) + row_shape, lambda i: (i, 0, 0)),
        out_shape=jax.ShapeDtypeStruct((n_tok,) + row_shape, F32),
        scratch_shapes=[pltpu.VMEM((tj * TOP_K,) + row_shape, F32),
                        pltpu.SemaphoreType.DMA(())],
        compiler_params=_params("arbitrary"),
    )(idx_flat, rank_flat, w_flat, pstart, y_rows)


def _final_kernel(hb_ref, r_ref, x1_ref, g2_ref, ng_ref, wgu_ref, wdn_ref, o_ref):
    tm, d = hb_ref.shape[1:]
    au = _dot(hb_ref[0], wgu_ref[...])
    a = au[:, :SHARED_DIM]
    u = au[:, SHARED_DIM:]
    shared = _dot((a * jax.nn.sigmoid(a) * u).astype(BF16), wdn_ref[...])
    routed = jnp.concatenate(
        [r_ref[pl.ds(s, tm, stride=SUBLANES), :] for s in range(d // LANES)], axis=1)
    o_ref[0] = x1_ref[0] + g2_ref[0] * _rms(routed + shared, ng_ref[...])


def _final(hb, routed2d, x1, g2, ng, wgu, wdn, tm):
    bsz, s, d = x1.shape
    nt = s // tm
    tile = pl.BlockSpec((1, tm, d), lambda bb, i: (bb, i, 0))
    return pl.pallas_call(
        _final_kernel,
        grid=(bsz, nt),
        in_specs=[tile,
                  pl.BlockSpec((tm * d // LANES, LANES), lambda bb, i: (bb * nt + i, 0)),
                  tile,
                  pl.BlockSpec((1, 1, d), lambda bb, i: (bb, 0, 0)),
                  pl.BlockSpec((1, d), lambda bb, i: (0, 0)),
                  pl.BlockSpec(wgu.shape, lambda bb, i: (0, 0)),
                  pl.BlockSpec(wdn.shape, lambda bb, i: (0, 0))],
        out_specs=tile,
        out_shape=jax.ShapeDtypeStruct((bsz, s, d), F32),
        compiler_params=_params("parallel", "parallel"),
    )(hb, routed2d, x1, g2, ng, wgu, wdn)


def _alibi_slopes(n):
    return (2.0 ** (-8.0 * jnp.arange(1, n + 1, dtype=F32) / n)).astype(F32)


def _layer(x, c, w_ada, b_ada, norm_g, w_in, w_cmp1, w_cmp2, pe_cmp, diff_lambda, diff_subln,
           w_out, w_router, router_bias, w_gate_up, w_down, ws_gate_up, ws_down, layer):
    bsz, s, d = x.shape
    n_tok = bsz * s
    tm = 256
    tq = 256
    tk = min(1024, s // 2)
    dk = HEAD_DIM
    g = NSA_KV_HEADS

    mod = _ada(c, w_ada, b_ada)
    sh1, sc1, g1, sh2, sc2, g2 = [m[:, None, :] for m in jnp.split(mod, 6, axis=-1)]

    n_pad = _round_up(IN_WIDTH, LANES)
    w_in_p = jnp.zeros((d, n_pad), BF16).at[:, :IN_WIDTH].set(w_in.astype(BF16))
    proj = _inproj(x, sc1, sh1, norm_g[0:1], w_in_p, tm)

    offs = [0]
    for wdt in IN_SPLITS:
        offs.append(offs[-1] + wdt)
    pieces = [proj[..., offs[j]:offs[j + 1]] for j in range(len(IN_SPLITS))]
    nq, kcm, vcm, ksel, vsel, kwin, vwin, ngate, dq, dkk, dv = pieces
    scale = dk ** -0.5

    def heads_first(a, nh):
        return a.reshape(bsz, s, nh, dk).transpose(0, 2, 1, 3)

    def aug_q(qh, slopes):
        shape = [1] * qh.ndim
        shape[1] = slopes.shape[0]
        sl = slopes.reshape(shape)
        feat = [jnp.broadcast_to(f, qh.shape[:-1] + (1,)).astype(BF16)
                for f in (sl * POS_SPLIT, sl, jnp.full_like(sl, NEG_INF))]
        pad = jnp.zeros(qh.shape[:-1] + (AUG_W - dk - len(feat),), BF16)
        return jnp.concatenate([qh] + feat + [pad], axis=-1)

    def aug_k(kh, block_onehot):
        col = jnp.arange(s, dtype=jnp.int32) % tk
        feat = [jnp.broadcast_to(f.astype(BF16)[:, None], kh.shape[:-1] + (1,))
                for f in (col // POS_SPLIT, col % POS_SPLIT)]
        pad = jnp.zeros(kh.shape[:-1] + (AUG_W - dk - len(feat),), BF16)
        parts = [kh] + feat + [pad]
        if block_onehot:
            oh = (jnp.arange(s)[:, None] // SEL_BLOCK == jnp.arange(LANES)[None, :]).astype(BF16)
            parts.append(jnp.broadcast_to(oh, kh.shape[:-1] + (LANES,)))
        return jnp.concatenate(parts, axis=-1)

    def pad_window(a, flag):
        a = jnp.pad(a, ((0, 0), (0, 0), (0, 0), (0, AUG_W - dk))) if flag else a
        a = jnp.pad(a, ((0, 0), (0, 0), (WINDOW, 0), (0, 0)))
        if flag:
            a = a.at[:, :, :WINDOW, AUG_PAD].set(1.0)
        return a

    q_nsa = aug_q(heads_first(nq * scale, NSA_HEADS), _alibi_slopes(NSA_HEADS))
    ks = aug_k(heads_first(ksel, g), True)
    vs = heads_first(vsel, g)
    kw = pad_window(heads_first(kwin, g), True)
    vw = pad_window(heads_first(vwin, g), False)
    gates = ngate.reshape(bsz, s, g, 3 * NSA_REP).transpose(0, 2, 1, 3)

    n_sub = CMP_BLOCK // CMP_STRIDE
    assert n_sub == 2
    ncp = s // CMP_STRIDE
    chunks = jnp.stack([heads_first(kcm, g), heads_first(vcm, g)]).reshape(
        2, bsz, g, ncp, CMP_STRIDE * dk)
    w1 = w_cmp1.astype(BF16).reshape(2, n_sub, CMP_STRIDE * dk, dk)
    pe8 = jnp.broadcast_to(pe_cmp.astype(BF16).reshape(2, 1, CMP_BLOCK * dk),
                           (2, SUBLANES, CMP_BLOCK * dk))
    cmp_kv = _compress(chunks, w1, w_cmp2.astype(BF16), pe8)

    n_blk = s // SEL_BLOCK
    assert n_blk <= LANES and min(SEL_TOPK, n_blk) == SEL_TOPK
    cs = jnp.arange(ncp)[:, None] * CMP_STRIDE
    bs = jnp.arange(LANES)[None, :] * SEL_BLOCK
    ov = (jnp.clip(jnp.minimum(cs + CMP_BLOCK, bs + SEL_BLOCK) - jnp.maximum(cs, bs), 0, None)
          .astype(F32) / CMP_BLOCK).astype(BF16)

    o_nsa = _nsa(_alibi_slopes(NSA_HEADS), q_nsa, cmp_kv[0], cmp_kv[1], ks, vs, kw, vw,
                 gates, ov, tq, tk)
    o_nsa = o_nsa.transpose(0, 2, 1, 3).reshape(bsz, s, NSA_Q_W)

    def maps_first(a):
        return a.reshape(bsz, s, DIFF_HEADS, 2, dk).transpose(0, 2, 3, 1, 4)

    lam_init = 0.8 - 0.6 * math.exp(-0.3 * layer)
    o_diff = _diff(_alibi_slopes(DIFF_HEADS),
                   aug_q(maps_first(dq * scale), _alibi_slopes(DIFF_HEADS)),
                   aug_k(maps_first(dkk), False),
                   dv.reshape(bsz, s, DIFF_HEADS, DIFF_V_DIM).transpose(0, 2, 1, 3),
                   diff_lambda, diff_subln, tq, tk, lam_init)

    x1, hb, h_rows2d, logits = _outproj(o_nsa, o_diff, w_out.astype(BF16), x, g1, sc2, sh2,
                                        norm_g[1:2], norm_g[2:3], w_router, tm)

    oi, ow, cnt = _router(logits.reshape(n_tok, N_EXPERTS), router_bias, tm)
    idx_flat = oi[:, :TOP_K].reshape(-1)
    rank_flat = oi[:, TOP_K:2 * TOP_K].reshape(-1)
    w_flat = ow[:, :TOP_K].reshape(-1)
    counts = cnt[0].astype(jnp.int32)
    padded = (counts + EXPERT_CHUNK - 1) // EXPERT_CHUNK * EXPERT_CHUNK
    pend = jnp.cumsum(padded)
    pstart = (pend - padded).astype(jnp.int32)
    n_asg = n_tok * TOP_K
    n_chunks = (n_asg + N_EXPERTS * (EXPERT_CHUNK - 1) + EXPERT_CHUNK - 1) // EXPERT_CHUNK
    n_rows = n_chunks * EXPERT_CHUNK
    chunk_e = jnp.minimum(
        jnp.searchsorted(pend, jnp.arange(n_chunks) * EXPERT_CHUNK, side='right'),
        N_EXPERTS - 1).astype(jnp.int32)
    n_active = (pend[-1:] // EXPERT_CHUNK).astype(jnp.int32)

    row_tile = (d // LANES, LANES)
    xs = _dispatch(idx_flat, rank_flat, pstart, pend.astype(jnp.int32),
                   h_rows2d.reshape((n_tok,) + row_tile), n_rows, 128)
    ys = _experts(chunk_e, n_active, xs.reshape(n_rows * d // LANES, LANES), w_gate_up, w_down)
    routed = _combine(idx_flat, rank_flat, w_flat, pstart, ys.reshape((n_rows,) + row_tile),
                      n_tok, 128)
    return _final(hb, routed.reshape(n_tok * d // LANES, LANES), x1, g2, norm_g[3:4],
                  ws_gate_up.astype(BF16), ws_down.astype(BF16), tm)


def kernel(x, c, w_ada, b_ada, norm_g, w_in, w_cmp1, w_cmp2, pe_cmp, diff_lambda, diff_subln,
           w_out, w_router, router_bias, w_gate_up, w_down, ws_gate_up, ws_down):
    for layer in range(w_ada.shape[0]):
        x = _layer(x, c, w_ada[layer], b_ada[layer], norm_g[layer], w_in[layer], w_cmp1[layer],
                   w_cmp2[layer], pe_cmp[layer], diff_lambda[layer], diff_subln[layer],
                   w_out[layer], w_router[layer], router_bias[layer], w_gate_up[layer],
                   w_down[layer], ws_gate_up[layer], ws_down[layer], layer)
    return x
```

```python
import functools
import math

import jax
import jax.numpy as jnp
from jax import lax
from jax.experimental import pallas as pl
from jax.experimental.pallas import tpu as pltpu

HEAD_DIM = 64
NSA_HEADS = 8
NSA_KV_HEADS = 2
NSA_REP = NSA_HEADS // NSA_KV_HEADS
CMP_BLOCK = 32
CMP_STRIDE = 16
SEL_BLOCK = 64
SEL_TOPK = 16
WINDOW = 512
DIFF_HEADS = 4
DIFF_V_DIM = 2 * HEAD_DIM
N_EXPERTS = 256
TOP_K = 8
N_GROUPS = 8
TOPK_GROUPS = 4
EXPERT_DIM = 256
SHARED_DIM = 256
ROUTED_SCALE = 2.5
EXPERT_CHUNK = 128
RMS_EPS = 1e-6
NEG_INF = -1e30
BELOW_NEG_INF = -3e38
FORCE_BONUS = 1e4

NSA_Q_W = NSA_HEADS * HEAD_DIM
NSA_KV_W = NSA_KV_HEADS * HEAD_DIM
NSA_GATE_W = 3 * NSA_HEADS
DIFF_QK_W = DIFF_HEADS * 2 * HEAD_DIM
DIFF_V_W = DIFF_HEADS * DIFF_V_DIM
IN_SPLITS = (NSA_Q_W,) + (NSA_KV_W,) * 6 + (NSA_GATE_W, DIFF_QK_W, DIFF_QK_W, DIFF_V_W)
IN_WIDTH = sum(IN_SPLITS)

LANES = 128
SUBLANES = 8
VMEM_LIMIT = 56 * 1024 * 1024
KEY_TILE = 1024

F32 = jnp.float32
BF16 = jnp.bfloat16


def _round_up(a, m):
    return (a + m - 1) // m * m


def _dot(a, b):
    return jnp.dot(a, b, preferred_element_type=F32)


def _dot_nt(a, b):
    return lax.dot_general(a, b, (((1,), (1,)), ((), ())), preferred_element_type=F32)


def _dot_split(a, b_bf16):
    hi = a.astype(BF16)
    lo = (a - hi.astype(F32)).astype(BF16)
    return _dot(hi, b_bf16) + _dot(lo, b_bf16)


def _dot_f32(a, b):
    a_hi = a.astype(BF16)
    a_lo = (a - a_hi.astype(F32)).astype(BF16)
    b_hi = b.astype(BF16)
    b_lo = (b - b_hi.astype(F32)).astype(BF16)
    return _dot(a_hi, b_hi) + (_dot(a_lo, b_hi) + _dot(a_hi, b_lo))


def _rms(x, g):
    return x * lax.rsqrt(jnp.mean(x * x, axis=-1, keepdims=True) + RMS_EPS) * g


def _params(*sem):
    return pltpu.CompilerParams(dimension_semantics=sem, vmem_limit_bytes=VMEM_LIMIT)


def _ada_kernel(c_ref, w_ref, b_ref, o_ref):
    c = c_ref[...]
    h = c * jax.nn.sigmoid(c)
    o_ref[...] = _dot_f32(h, w_ref[...]) + b_ref[...]


def _ada(c, w, b):
    bsz, d = c.shape
    n = w.shape[1]
    tn = n // 4
    cp = jnp.zeros((SUBLANES, d), F32).at[:bsz].set(c)
    out = pl.pallas_call(
        _ada_kernel,
        grid=(n // tn,),
        in_specs=[pl.BlockSpec((SUBLANES, d), lambda j: (0, 0)),
                  pl.BlockSpec((d, tn), lambda j: (0, j)),
                  pl.BlockSpec((1, tn), lambda j: (0, j))],
        out_specs=pl.BlockSpec((SUBLANES, tn), lambda j: (0, j)),
        out_shape=jax.ShapeDtypeStruct((SUBLANES, n), F32),
        compiler_params=_params("arbitrary"),
    )(cp, w, b.reshape(1, n))
    return out[:bsz]


def _inproj_kernel(x_ref, sc_ref, sh_ref, g_ref, w_ref, o_ref):
    h = _rms(x_ref[0], g_ref[...]) * (1.0 + sc_ref[0]) + sh_ref[0]
    o_ref[0] = _dot(h.astype(BF16), w_ref[...]).astype(o_ref.dtype)


def _inproj(x, sc, sh, g, w_bf16, tm):
    bsz, s, d = x.shape
    n = w_bf16.shape[1]
    return pl.pallas_call(
        _inproj_kernel,
        grid=(bsz, s // tm),
        in_specs=[pl.BlockSpec((1, tm, d), lambda b, i: (b, i, 0)),
                  pl.BlockSpec((1, 1, d), lambda b, i: (b, 0, 0)),
                  pl.BlockSpec((1, 1, d), lambda b, i: (b, 0, 0)),
                  pl.BlockSpec((1, d), lambda b, i: (0, 0)),
                  pl.BlockSpec((d, n), lambda b, i: (0, 0))],
        out_specs=pl.BlockSpec((1, tm, n), lambda b, i: (b, i, 0)),
        out_shape=jax.ShapeDtypeStruct((bsz, s, n), BF16),
        compiler_params=_params("parallel", "parallel"),
    )(x, sc, sh, g, w_bf16)


def _cmp_kernel(ch_ref, w1_ref, w2_ref, pe_ref, o_ref):
    ch = ch_ref[0, 0, 0]
    half = ch.shape[1]
    a = _dot(ch, w1_ref[0, 0])
    bm = _dot(ch, w1_ref[0, 1])
    n = bm.shape[0]
    bm_next = pltpu.roll(bm, shift=n - 1, axis=0)
    pe = pe_ref[0]
    pt = _dot(pe[:, :half], w1_ref[0, 0]) + _dot(pe[:, half:], w1_ref[0, 1])
    hmid = jax.nn.gelu(a + bm_next + pt[0:1])
    o_ref[0, 0, 0] = _dot(hmid.astype(BF16), w2_ref[0]).astype(o_ref.dtype)


def _compress(chunks, w1, w2, pe8):
    two, bsz, g, ncp, cw = chunks.shape
    dk = w2.shape[-1]
    return pl.pallas_call(
        _cmp_kernel,
        grid=(two, bsz, g),
        in_specs=[pl.BlockSpec((1, 1, 1, ncp, cw), lambda a, b, c: (a, b, c, 0, 0)),
                  pl.BlockSpec((1, 2, cw, dk), lambda a, b, c: (a, 0, 0, 0)),
                  pl.BlockSpec((1, dk, dk), lambda a, b, c: (a, 0, 0)),
                  pl.BlockSpec((1, SUBLANES, 2 * cw), lambda a, b, c: (a, 0, 0))],
        out_specs=pl.BlockSpec((1, 1, 1, ncp, dk), lambda a, b, c: (a, b, c, 0, 0)),
        out_shape=jax.ShapeDtypeStruct((two, bsz, g, ncp, dk), BF16),
        compiler_params=_params("parallel", "parallel", "parallel"),
    )(chunks, w1, w2, pe8)


AUG_W = LANES
POS_SPLIT = 64


def _flash_tile(s, delta, carry, v):
    m, l, acc = carry
    m_new = jnp.maximum(m, jnp.max(s, axis=-1, keepdims=True) + delta)
    pr = jnp.exp(s - (m_new - delta))
    alpha = jnp.exp(m - m_new)
    return (m_new, alpha * l + jnp.sum(pr, axis=-1, keepdims=True),
            alpha * acc + _dot(pr.astype(BF16), v))


def _nsa_kernel(slopes_ref, q_ref, kc_ref, vc_ref, ks_ref, vs_ref, kw_ref, vw_ref,
                g_ref, ov_ref, o_ref, bw_ref, *, tq, tk):
    g = pl.program_id(1)
    i = pl.program_id(2)
    rep = NSA_REP
    rows = rep * tq
    s0 = i * tq
    qa = q_ref[0].reshape(rows, AUG_W)
    q4 = qa[:, :HEAD_DIM]
    t = s0 + lax.broadcasted_iota(jnp.int32, (tq, 1), 0)
    t4 = jnp.concatenate([t] * rep, axis=0)
    slope4 = jnp.concatenate(
        [jnp.full((tq, 1), slopes_ref[g * rep + r], F32) for r in range(rep)], axis=0)
    wl = WINDOW + tq

    @pl.when(i <= WINDOW // tq)
    def _():
        row = lax.broadcasted_iota(jnp.int32, (tq, 1), 0)
        dw = (jnp.concatenate([row] * rep, axis=0)
              + (jnp.minimum(s0, WINDOW) - lax.broadcasted_iota(jnp.int32, (1, wl), 1)))
        bw_ref[...] = jnp.where((dw >= 0) & (dw < WINDOW), -slope4 * dw.astype(F32), NEG_INF)

    kc = kc_ref[0, 0]
    ncp = kc.shape[0]
    sc = _dot_nt(q4, kc)
    cend = lax.broadcasted_iota(jnp.int32, (1, ncp), 1) * CMP_STRIDE + (CMP_BLOCK - 1)
    dist = t4 - cend
    vis = dist >= 0
    sc = jnp.where(vis, sc - slope4 * dist.astype(F32), NEG_INF)
    mx = jnp.max(sc, axis=-1, keepdims=True)
    e = jnp.where(vis, jnp.exp(sc - mx), 0.0)
    p = e / jnp.maximum(jnp.sum(e, axis=-1, keepdims=True), 1e-30)
    o_c = _dot(p.astype(BF16), vc_ref[0, 0])

    psum = p[0:tq]
    for r in range(1, rep):
        psum = psum + p[r * tq:(r + 1) * tq]
    imp = _dot_split(psum, ov_ref[...])
    jf = lax.broadcasted_iota(jnp.int32, (tq, LANES), 1)
    qb = t // SEL_BLOCK
    forced = (jf == 0) | (jf == qb) | (jf == qb - 1)
    v = jnp.where(jf <= qb, imp + jnp.where(forced, FORCE_BONUS, 0.0), NEG_INF)
    jff = jf.astype(F32)
    selb = jnp.full((tq, LANES), NEG_INF, F32)
    for _ in range(SEL_TOPK):
        vmax = jnp.max(v, axis=-1, keepdims=True)
        first = jnp.min(jnp.where(v == vmax, jff, float(LANES)), axis=-1, keepdims=True)
        pick = jff == first
        selb = jnp.where(pick, 0.0, selb)
        v = jnp.where(pick, BELOW_NEG_INF, v)
    chosen = jnp.max(jnp.where(selb == 0.0, 1.0, 0.0), axis=0, keepdims=True)
    selb = selb.astype(BF16)

    qs = jnp.concatenate([qa, jnp.concatenate([selb] * rep, axis=0)], axis=1)

    def sel_tile(kt, carry, masked):
        k0 = pl.multiple_of(kt * tk, tk)
        s = _dot_nt(qs, ks_ref[0, 0, pl.ds(k0, tk), :])
        if masked:
            s = jnp.where(t4 >= k0 + lax.broadcasted_iota(jnp.int32, (1, tk), 1), s, NEG_INF)
        delta = slope4 * (k0 - s0).astype(F32)
        return _flash_tile(s, delta, carry, vs_ref[0, 0, pl.ds(k0, tk), :])

    init = (jnp.full((rows, 1), NEG_INF, F32), jnp.zeros((rows, 1), F32),
            jnp.zeros((rows, HEAD_DIM), F32))
    blk_lane = lax.broadcasted_iota(jnp.int32, (1, LANES), 1)
    blk_per_tile = tk // SEL_BLOCK

    def sel_full_tile(kt, carry):
        in_tile = (blk_lane >= kt * blk_per_tile) & (blk_lane < (kt + 1) * blk_per_tile)
        hit = jnp.max(jnp.where(in_tile, chosen, 0.0)) > 0.0
        return lax.cond(hit, lambda c: sel_tile(kt, c, False), lambda c: c, carry)

    n_full = s0 // tk
    carry = lax.fori_loop(0, n_full, sel_full_tile, init)
    _, l_s, acc_s = sel_tile(n_full, carry, True)
    o_s = acc_s / l_s

    ws = pl.multiple_of(jnp.maximum(s0 - WINDOW, 0), tq)
    sw = _dot_nt(q4, kw_ref[0, 0, pl.ds(ws, wl), :]) + bw_ref[...]
    ew = jnp.exp(sw - jnp.max(sw, axis=-1, keepdims=True))
    o_w = _dot(ew.astype(BF16), vw_ref[0, 0, pl.ds(ws, wl), :]) / jnp.sum(ew, axis=-1, keepdims=True)

    gate = jax.nn.sigmoid(g_ref[0, 0].astype(F32))
    for r in range(rep):
        rr = slice(r * tq, (r + 1) * tq)
        o = (gate[:, 3 * r:3 * r + 1] * o_c[rr] + gate[:, 3 * r + 1:3 * r + 2] * o_s[rr]
             + gate[:, 3 * r + 2:3 * r + 3] * o_w[rr])
        o_ref[0, r] = o.astype(o_ref.dtype)


def _nsa(slopes, q, kc, vc, ks, vs, kw, vw, gates, ov, tq, tk):
    bsz, h, s, _ = q.shape
    g = NSA_KV_HEADS
    rep = NSA_REP
    dk = HEAD_DIM
    assert s % tk == 0 and tk % tq == 0 and tq % SEL_BLOCK == 0 and tk // POS_SPLIT <= 256
    assert s >= WINDOW + tq and WINDOW % tq == 0

    def resident(a):
        return pl.BlockSpec((1, 1) + a.shape[2:], lambda b, c, i: (b, c, 0, 0))

    return pl.pallas_call(
        functools.partial(_nsa_kernel, tq=tq, tk=tk),
        grid=(bsz, g, s // tq),
        in_specs=[pl.BlockSpec(memory_space=pltpu.SMEM),
                  pl.BlockSpec((1, rep, tq, AUG_W), lambda b, c, i: (b, c, i, 0)),
                  resident(kc), resident(vc), resident(ks), resident(vs), resident(kw), resident(vw),
                  pl.BlockSpec((1, 1, tq, 3 * rep), lambda b, c, i: (b, c, i, 0)),
                  pl.BlockSpec(ov.shape, lambda b, c, i: (0, 0))],
        out_specs=pl.BlockSpec((1, rep, tq, dk), lambda b, c, i: (b, c, i, 0)),
        out_shape=jax.ShapeDtypeStruct((bsz, h, s, dk), BF16),
        scratch_shapes=[pltpu.VMEM((rep * tq, WINDOW + tq), F32)],
        compiler_params=_params("parallel", "parallel", "arbitrary"),
    )(slopes, q, kc, vc, ks, vs, kw, vw, gates, ov)


def _diff_kernel(slopes_ref, q_ref, k_ref, v_ref, lam_ref, g_ref, o_ref, *, tq, tk, lam_init):
    h = pl.program_id(1)
    i = pl.program_id(2)
    s0 = i * tq
    slope = slopes_ref[h]
    t = s0 + lax.broadcasted_iota(jnp.int32, (tq, 1), 0)

    def tile(kt, carry, masked):
        k0 = pl.multiple_of(kt * tk, tk)
        delta = slope * (k0 - s0).astype(F32)
        v = v_ref[0, 0, pl.ds(k0, tk), :]
        new = []
        for mi in range(2):
            s = _dot_nt(q_ref[0, 0, mi], k_ref[0, 0, mi, pl.ds(k0, tk), :])
            if masked:
                s = jnp.where(t >= k0 + lax.broadcasted_iota(jnp.int32, (1, tk), 1), s, NEG_INF)
            new.append(_flash_tile(s, delta, carry[mi], v))
        return tuple(new)

    one = (jnp.full((tq, 1), NEG_INF, F32), jnp.zeros((tq, 1), F32),
           jnp.zeros((tq, DIFF_V_DIM), F32))
    n_full = s0 // tk
    carry = lax.fori_loop(0, n_full, functools.partial(tile, masked=False), (one, one))
    (_, l0, acc0), (_, l1, acc1) = tile(n_full, carry, True)
    lp = lam_ref[...]
    lam = (jnp.exp(jnp.sum(lp[0:1] * lp[1:2], axis=-1, keepdims=True))
           - jnp.exp(jnp.sum(lp[2:3] * lp[3:4], axis=-1, keepdims=True)) + lam_init)
    o = acc0 / l0 - lam * (acc1 / l1)
    o_ref[0] = (_rms(o, g_ref[...]) * (1.0 - lam_init)).astype(o_ref.dtype)


def _diff(slopes, q, k, v, lam_params, subln_g, tq, tk, lam_init):
    bsz, h, _, s, _ = q.shape
    dv = v.shape[-1]
    assert s % tk == 0 and tk % tq == 0 and tk // POS_SPLIT <= 256
    return pl.pallas_call(
        functools.partial(_diff_kernel, tq=tq, tk=tk, lam_init=lam_init),
        grid=(bsz, h, s // tq),
        in_specs=[pl.BlockSpec(memory_space=pltpu.SMEM),
                  pl.BlockSpec((1, 1, 2, tq, AUG_W), lambda b, c, i: (b, c, 0, i, 0)),
                  pl.BlockSpec((1, 1, 2, s, AUG_W), lambda b, c, i: (b, c, 0, 0, 0)),
                  pl.BlockSpec((1, 1, s, dv), lambda b, c, i: (b, c, 0, 0)),
                  pl.BlockSpec(lam_params.shape, lambda b, c, i: (0, 0)),
                  pl.BlockSpec((1, dv), lambda b, c, i: (0, 0))],
        out_specs=pl.BlockSpec((1, tq, dv), lambda b, c, i: (b, i, c)),
        out_shape=jax.ShapeDtypeStruct((bsz, s, h * dv), BF16),
        compiler_params=_params("parallel", "parallel", "arbitrary"),
    )(slopes, q, k, v, lam_params, subln_g.reshape(1, dv))


def _outproj_kernel(a_ref, b_ref, wo_ref, x_ref, g1_ref, sc_ref, sh_ref, ng1_ref, ng2_ref,
                    wr_ref, x1_ref, hb_ref, hrow_ref, lg_ref):
    half = a_ref.shape[-1]
    tm = a_ref.shape[1]
    y = _dot(a_ref[0], wo_ref[0:half, :]) + _dot(b_ref[0], wo_ref[half:, :])
    x1 = x_ref[0] + g1_ref[0] * _rms(y, ng1_ref[...])
    x1_ref[0] = x1
    h = _rms(x1, ng2_ref[...]) * (1.0 + sc_ref[0]) + sh_ref[0]
    hb_ref[0] = h.astype(BF16)
    for s in range(h.shape[1] // LANES):
        hrow_ref[pl.ds(s, tm, stride=SUBLANES), :] = h[:, s * LANES:(s + 1) * LANES]
    lg_ref[0] = _dot_f32(h, wr_ref[...])


def _outproj(a, b, wo, x, g1, sc2, sh2, ng1, ng2, wr, tm):
    bsz, s, d = x.shape
    half = a.shape[-1]
    ne = wr.shape[1]
    nt = s // tm
    vec = pl.BlockSpec((1, 1, d), lambda bb, i: (bb, 0, 0))
    row = pl.BlockSpec((1, d), lambda bb, i: (0, 0))
    tile = pl.BlockSpec((1, tm, d), lambda bb, i: (bb, i, 0))
    return pl.pallas_call(
        _outproj_kernel,
        grid=(bsz, nt),
        in_specs=[pl.BlockSpec((1, tm, half), lambda bb, i: (bb, i, 0)),
                  pl.BlockSpec((1, tm, half), lambda bb, i: (bb, i, 0)),
                  pl.BlockSpec((d, d), lambda bb, i: (0, 0)),
                  tile, vec, vec, vec, row, row,
                  pl.BlockSpec((d, ne), lambda bb, i: (0, 0))],
        out_specs=[tile, tile,
                   pl.BlockSpec((tm * d // LANES, LANES), lambda bb, i: (bb * nt + i, 0)),
                   pl.BlockSpec((1, tm, ne), lambda bb, i: (bb, i, 0))],
        out_shape=[jax.ShapeDtypeStruct((bsz, s, d), F32),
                   jax.ShapeDtypeStruct((bsz, s, d), BF16),
                   jax.ShapeDtypeStruct((bsz * s * d // LANES, LANES), F32),
                   jax.ShapeDtypeStruct((bsz, s, ne), F32)],
        compiler_params=_params("parallel", "parallel"),
    )(a, b, wo, x, g1, sc2, sh2, ng1, ng2, wr)


def _router_kernel(lg_ref, bias_ref, tri_ref, oi_ref, ow_ref, cnt_ref, carry_ref):
    @pl.when(pl.program_id(0) == 0)
    def _():
        carry_ref[...] = jnp.zeros_like(carry_ref)

    tm, ne = lg_ref.shape
    per_grp = ne // N_GROUPS
    aff = jax.nn.sigmoid(lg_ref[...])
    choice = aff + bias_ref[...]
    lane = lax.broadcasted_iota(jnp.int32, (tm, ne), 1)
    lanef = lane.astype(F32)
    grp = lane // per_grp

    def row_max(a):
        return jnp.max(a, axis=-1, keepdims=True)

    def first_lane(eq):
        return jnp.min(jnp.where(eq, lanef, float(ne)), axis=-1, keepdims=True)

    gscore = []
    for gi in range(N_GROUPS):
        mg = jnp.where(grp == gi, choice, BELOW_NEG_INF)
        m1 = row_max(mg)
        mg = jnp.where(lanef == first_lane(mg == m1), BELOW_NEG_INF, mg)
        gscore.append(m1 + row_max(mg))
    keep = jnp.zeros((tm, ne), F32)
    for gi in range(N_GROUPS):
        beaten = jnp.zeros((tm, 1), F32)
        for gj in range(N_GROUPS):
            if gj == gi:
                continue
            wins = gscore[gj] >= gscore[gi] if gj < gi else gscore[gj] > gscore[gi]
            beaten = beaten + jnp.where(wins, 1.0, 0.0)
        keep = jnp.where(grp == gi, jnp.where(beaten < TOPK_GROUPS, 1.0, 0.0), keep)
    ch = jnp.where(keep > 0.5, choice, NEG_INF)

    idx, wts = [], []
    onehot = jnp.zeros((tm, ne), F32)
    for _ in range(TOP_K):
        first = first_lane(ch == row_max(ch))
        pick = lanef == first
        wts.append(jnp.sum(jnp.where(pick, aff, 0.0), axis=-1, keepdims=True))
        ch = jnp.where(pick, BELOW_NEG_INF, ch)
        onehot = jnp.where(pick, 1.0, onehot)
        idx.append(first)
    wsum = wts[0]
    for w in wts[1:]:
        wsum = wsum + w
    before = carry_ref[...] + _dot(tri_ref[...], onehot.astype(BF16))
    carry_ref[...] = carry_ref[...] + jnp.sum(onehot, axis=0, keepdims=True)
    cnt_ref[...] = jnp.broadcast_to(carry_ref[...], cnt_ref.shape)

    l128 = lax.broadcasted_iota(jnp.int32, (tm, LANES), 1)
    oi = jnp.zeros((tm, LANES), jnp.int32)
    ow = jnp.zeros((tm, LANES), F32)
    for k in range(TOP_K):
        rank = jnp.sum(jnp.where(lanef == idx[k], before, 0.0), axis=-1, keepdims=True)
        oi = jnp.where(l128 == k, idx[k].astype(jnp.int32), oi)
        oi = jnp.where(l128 == TOP_K + k, rank.astype(jnp.int32), oi)
        ow = jnp.where(l128 == k, wts[k] / wsum * ROUTED_SCALE, ow)
    oi_ref[...] = oi
    ow_ref[...] = ow


def _router(logits, bias, tm):
    n_tok, ne = logits.shape
    tri = (jnp.arange(tm)[:, None] > jnp.arange(tm)[None, :]).astype(BF16)
    return pl.pallas_call(
        _router_kernel,
        grid=(n_tok // tm,),
        in_specs=[pl.BlockSpec((tm, ne), lambda i: (i, 0)),
                  pl.BlockSpec((1, ne), lambda i: (0, 0)),
                  pl.BlockSpec((tm, tm), lambda i: (0, 0))],
        out_specs=[pl.BlockSpec((tm, LANES), lambda i: (i, 0)),
                   pl.BlockSpec((tm, LANES), lambda i: (i, 0)),
                   pl.BlockSpec((SUBLANES, ne), lambda i: (0, 0))],
        out_shape=[jax.ShapeDtypeStruct((n_tok, LANES), jnp.int32),
                   jax.ShapeDtypeStruct((n_tok, LANES), F32),
                   jax.ShapeDtypeStruct((SUBLANES, ne), F32)],
        scratch_shapes=[pltpu.VMEM((1, ne), F32)],
        compiler_params=_params("arbitrary"),
    )(logits, bias.reshape(1, ne), tri)


def _row_copy(src, src_row, dst, dst_row, sem):
    return pltpu.make_async_copy(src.at[src_row], dst.at[dst_row], sem)


def _chunk_fill(zbuf, xs_ref, start, sem):
    return pltpu.make_async_copy(zbuf, xs_ref.at[pl.ds(start, EXPERT_CHUNK)], sem)


def _dispatch_kernel(idx_ref, rank_ref, pstart_ref, pend_ref, h_ref, xs_ref, zbuf, sem, zsem, *, th):
    @pl.when(pl.program_id(0) == 0)
    def _():
        zbuf[...] = jnp.zeros_like(zbuf)

        def fill(e, carry):
            @pl.when(pend_ref[e] > pstart_ref[e])
            def _():
                _chunk_fill(zbuf, xs_ref, pend_ref[e] - EXPERT_CHUNK, zsem).start()
            return carry

        lax.fori_loop(0, N_EXPERTS, fill, 0)

        def fill_done(e, carry):
            @pl.when(pend_ref[e] > pstart_ref[e])
            def _():
                _chunk_fill(zbuf, xs_ref, 0, zsem).wait()
            return carry

        lax.fori_loop(0, N_EXPERTS, fill_done, 0)

        first_unused = pend_ref[N_EXPERTS - 1] // EXPERT_CHUNK
        n_chunks = xs_ref.shape[0] // EXPERT_CHUNK

        def tail(ci, carry):
            _chunk_fill(zbuf, xs_ref, ci * EXPERT_CHUNK, zsem).start()
            return carry

        lax.fori_loop(first_unused, n_chunks, tail, 0)

        def tail_done(ci, carry):
            _chunk_fill(zbuf, xs_ref, 0, zsem).wait()
            return carry

        lax.fori_loop(first_unused, n_chunks, tail_done, 0)

    def issue(tl, carry):
        for k in range(TOP_K):
            a = tl * TOP_K + k
            dst = pstart_ref[idx_ref[a]] + rank_ref[a]
            _row_copy(h_ref, tl, xs_ref, dst, sem).start()
        return carry

    lax.fori_loop(0, th, issue, 0)

    def drain(a, carry):
        _row_copy(h_ref, 0, xs_ref, 0, sem).wait()
        return carry

    lax.fori_loop(0, th * TOP_K, drain, 0)


def _dispatch(idx_flat, rank_flat, pstart, pend, h_rows, n_rows, th):
    n_tok = h_rows.shape[0]
    row_shape = h_rows.shape[1:]
    smem_blk = pl.BlockSpec((th * TOP_K,), lambda i: (i,), memory_space=pltpu.SMEM)
    return pl.pallas_call(
        functools.partial(_dispatch_kernel, th=th),
        grid=(n_tok // th,),
        in_specs=[smem_blk, smem_blk,
                  pl.BlockSpec(memory_space=pltpu.SMEM),
                  pl.BlockSpec(memory_space=pltpu.SMEM),
                  pl.BlockSpec((th,) + row_shape, lambda i: (i, 0, 0))],
        out_specs=pl.BlockSpec(memory_space=pl.ANY),
        out_shape=jax.ShapeDtypeStruct((n_rows,) + row_shape, h_rows.dtype),
        scratch_shapes=[pltpu.VMEM((EXPERT_CHUNK,) + row_shape, h_rows.dtype),
                        pltpu.SemaphoreType.DMA(()), pltpu.SemaphoreType.DMA(())],
        compiler_params=_params("arbitrary"),
    )(idx_flat, rank_flat, pstart, pend, h_rows)


def _weight_fetch(wgu_hbm, wdn_hbm, gu_buf, dn_buf, sems, expert, slot):
    return (pltpu.make_async_copy(wgu_hbm.at[expert], gu_buf.at[slot], sems.at[slot, 0]),
            pltpu.make_async_copy(wdn_hbm.at[expert], dn_buf.at[slot], sems.at[slot, 1]))


def _experts_kernel(ce_ref, first_ref, next_ref, slot_ref, na_ref, x_ref, wgu_hbm, wdn_hbm, o_ref,
                    gu_buf, dn_buf, gu_bf, dn_bf, sems):
    i = pl.program_id(0)
    ch = EXPERT_CHUNK
    nsl = x_ref.shape[0] // ch
    active = i < na_ref[0]
    fetch = functools.partial(_weight_fetch, wgu_hbm, wdn_hbm, gu_buf, dn_buf, sems)

    @pl.when(i == 0)
    def _():
        for cp in fetch(ce_ref[0], 0):
            cp.start()

    @pl.when(active & (first_ref[i] == 1))
    def _():
        slot = slot_ref[i]
        for cp in fetch(ce_ref[i], slot):
            cp.wait()

        @pl.when(next_ref[i] >= 0)
        def _():
            for cp in fetch(next_ref[i], 1 - slot):
                cp.start()

        gu_bf[...] = gu_buf[slot].astype(BF16)
        dn_bf[...] = dn_buf[slot].astype(BF16)

    @pl.when(active)
    def _():
        x = jnp.concatenate(
            [x_ref[pl.ds(s, ch, stride=SUBLANES), :] for s in range(nsl)], axis=1).astype(BF16)
        au = _dot(x, gu_bf[...])
        a = au[:, :EXPERT_DIM]
        u = au[:, EXPERT_DIM:]
        hmid = (a * jax.nn.sigmoid(a) * u).astype(BF16)
        y = _dot(hmid, dn_bf[...])
        for s in range(nsl):
            o_ref[pl.ds(s, ch, stride=SUBLANES), :] = y[:, s * LANES:(s + 1) * LANES]

    @pl.when(jnp.logical_not(active))
    def _():
        o_ref[...] = jnp.zeros_like(o_ref)


def _experts(chunk_e, first, next_e, slot, n_active, xs2d, w_gu, w_dn):
    n_chunks = chunk_e.shape[0]
    _, d, two_e = w_gu.shape
    blk_rows = EXPERT_CHUNK * d // LANES

    def x_map(i, ce, fs, nx, sl, na):
        return (jnp.minimum(i, na[0] - 1), 0)

    grid_spec = pltpu.PrefetchScalarGridSpec(
        num_scalar_prefetch=5,
        grid=(n_chunks,),
        in_specs=[pl.BlockSpec((blk_rows, LANES), x_map),
                  pl.BlockSpec(memory_space=pl.ANY),
                  pl.BlockSpec(memory_space=pl.ANY)],
        out_specs=pl.BlockSpec((blk_rows, LANES), lambda i, ce, fs, nx, sl, na: (i, 0)),
        scratch_shapes=[pltpu.VMEM((2, d, two_e), F32), pltpu.VMEM((2, two_e // 2, d), F32),
                        pltpu.VMEM((d, two_e), BF16), pltpu.VMEM((two_e // 2, d), BF16),
                        pltpu.SemaphoreType.DMA((2, 2))],
    )
    return pl.pallas_call(
        _experts_kernel,
        grid_spec=grid_spec,
        out_shape=jax.ShapeDtypeStruct(xs2d.shape, F32),
        compiler_params=_params("arbitrary"),
    )(chunk_e, first, next_e, slot, n_active, xs2d, w_gu, w_dn)


def _combine_kernel(idx_ref, rank_ref, w_ref, pstart_ref, y_ref, o_ref, buf, sem, *, tj):
    def issue(tl, carry):
        for k in range(TOP_K):
            a = tl * TOP_K + k
            src = pstart_ref[idx_ref[a]] + rank_ref[a]
            _row_copy(y_ref, src, buf, a, sem).start()
        return carry

    lax.fori_loop(0, tj, issue, 0)

    def drain(a, carry):
        _row_copy(y_ref, 0, buf, 0, sem).wait()
        return carry

    lax.fori_loop(0, tj * TOP_K, drain, 0)

    def reduce(tl, carry):
        acc = w_ref[tl * TOP_K] * buf[tl * TOP_K]
        for k in range(1, TOP_K):
            acc = acc + w_ref[tl * TOP_K + k] * buf[tl * TOP_K + k]
        o_ref[tl] = acc
        return carry

    lax.fori_loop(0, tj, reduce, 0)


def _combine(idx_flat, rank_flat, w_flat, pstart, y_rows, n_tok, tj):
    row_shape = y_rows.shape[1:]
    smem_blk = pl.BlockSpec((tj * TOP_K,), lambda i: (i,), memory_space=pltpu.SMEM)
    return pl.pallas_call(
        functools.partial(_combine_kernel, tj=tj),
        grid=(n_tok // tj,),
        in_specs=[smem_blk, smem_blk, smem_blk,
                  pl.BlockSpec(memory_space=pltpu.SMEM),
                  pl.BlockSpec(memory_space=pl.ANY)],
        out_specs=pl.BlockSpec((tj,) + row_shape, lambda i: (i, 0, 0)),
        out_shape=jax.ShapeDtypeStruct((n_tok,) + row_shape, F32),
        scratch_shapes=[pltpu.VMEM((tj * TOP_K,) + row_shape, F32),
                        pltpu.SemaphoreType.DMA(())],
        compiler_params=_params("arbitrary"),
    )(idx_flat, rank_flat, w_flat, pstart, y_rows)


def _final_kernel(hb_ref, r_ref, x1_ref, g2_ref, ng_ref, wgu_ref, wdn_ref, o_ref):
    tm, d = hb_ref.shape[1:]
    au = _dot(hb_ref[0], wgu_ref[...])
    a = au[:, :SHARED_DIM]
    u = au[:, SHARED_DIM:]
    shared = _dot((a * jax.nn.sigmoid(a) * u).astype(BF16), wdn_ref[...])
    routed = jnp.concatenate(
        [r_ref[pl.ds(s, tm, stride=SUBLANES), :] for s in range(d // LANES)], axis=1)
    o_ref[0] = x1_ref[0] + g2_ref[0] * _rms(routed + shared, ng_ref[...])


def _final(hb, routed2d, x1, g2, ng, wgu, wdn, tm):
    bsz, s, d = x1.shape
    nt = s // tm
    tile = pl.BlockSpec((1, tm, d), lambda bb, i: (bb, i, 0))
    return pl.pallas_call(
        _final_kernel,
        grid=(bsz, nt),
        in_specs=[tile,
                  pl.BlockSpec((tm * d // LANES, LANES), lambda bb, i: (bb * nt + i, 0)),
                  tile,
                  pl.BlockSpec((1, 1, d), lambda bb, i: (bb, 0, 0)),
                  pl.BlockSpec((1, d), lambda bb, i: (0, 0)),
                  pl.BlockSpec(wgu.shape, lambda bb, i: (0, 0)),
                  pl.BlockSpec(wdn.shape, lambda bb, i: (0, 0))],
        out_specs=tile,
        out_shape=jax.ShapeDtypeStruct((bsz, s, d), F32),
        compiler_params=_params("parallel", "parallel"),
    )(hb, routed2d, x1, g2, ng, wgu, wdn)


def _alibi_slopes(n):
    return (2.0 ** (-8.0 * jnp.arange(1, n + 1, dtype=F32) / n)).astype(F32)


def _layer(x, c, w_ada, b_ada, norm_g, w_in, w_cmp1, w_cmp2, pe_cmp, diff_lambda, diff_subln,
           w_out, w_router, router_bias, w_gate_up, w_down, ws_gate_up, ws_down, layer):
    bsz, s, d = x.shape
    n_tok = bsz * s
    tm = 256
    tq = 256
    tk = min(KEY_TILE, s // 2)
    dk = HEAD_DIM
    g = NSA_KV_HEADS

    mod = _ada(c, w_ada, b_ada)
    sh1, sc1, g1, sh2, sc2, g2 = [m[:, None, :] for m in jnp.split(mod, 6, axis=-1)]

    n_pad = _round_up(IN_WIDTH, LANES)
    w_in_p = jnp.zeros((d, n_pad), BF16).at[:, :IN_WIDTH].set(w_in.astype(BF16))
    proj = _inproj(x, sc1, sh1, norm_g[0:1], w_in_p, tm)

    offs = [0]
    for wdt in IN_SPLITS:
        offs.append(offs[-1] + wdt)
    pieces = [proj[..., offs[j]:offs[j + 1]] for j in range(len(IN_SPLITS))]
    nq, kcm, vcm, ksel, vsel, kwin, vwin, ngate, dq, dkk, dv = pieces
    scale = dk ** -0.5

    def heads_first(a, nh):
        return a.reshape(bsz, s, nh, dk).transpose(0, 2, 1, 3)

    def aug_q(qh, slopes):
        shape = [1] * qh.ndim
        shape[1] = slopes.shape[0]
        sl = slopes.reshape(shape)
        feat = [jnp.broadcast_to(f, qh.shape[:-1] + (1,)).astype(BF16) for f in (sl * POS_SPLIT, sl)]
        pad = jnp.zeros(qh.shape[:-1] + (AUG_W - dk - len(feat),), BF16)
        return jnp.concatenate([qh] + feat + [pad], axis=-1)

    def aug_k(kh, block_onehot):
        col = jnp.arange(s, dtype=jnp.int32) % tk
        feat = [jnp.broadcast_to(f.astype(BF16)[:, None], kh.shape[:-1] + (1,))
                for f in (col // POS_SPLIT, col % POS_SPLIT)]
        pad = jnp.zeros(kh.shape[:-1] + (AUG_W - dk - len(feat),), BF16)
        parts = [kh] + feat + [pad]
        if block_onehot:
            oh = (jnp.arange(s)[:, None] // SEL_BLOCK == jnp.arange(LANES)[None, :]).astype(BF16)
            parts.append(jnp.broadcast_to(oh, kh.shape[:-1] + (LANES,)))
        return jnp.concatenate(parts, axis=-1)

    q_nsa = aug_q(heads_first(nq * scale, NSA_HEADS), _alibi_slopes(NSA_HEADS))
    ks = aug_k(heads_first(ksel, g), True)
    vs, kw, vw = [heads_first(a, g) for a in (vsel, kwin, vwin)]
    gates = ngate.reshape(bsz, s, g, 3 * NSA_REP).transpose(0, 2, 1, 3)

    n_sub = CMP_BLOCK // CMP_STRIDE
    assert n_sub == 2
    ncp = s // CMP_STRIDE
    chunks = jnp.stack([heads_first(kcm, g), heads_first(vcm, g)]).reshape(
        2, bsz, g, ncp, CMP_STRIDE * dk)
    w1 = w_cmp1.astype(BF16).reshape(2, n_sub, CMP_STRIDE * dk, dk)
    pe8 = jnp.broadcast_to(pe_cmp.astype(BF16).reshape(2, 1, CMP_BLOCK * dk),
                           (2, SUBLANES, CMP_BLOCK * dk))
    cmp_kv = _compress(chunks, w1, w_cmp2.astype(BF16), pe8)

    n_blk = s // SEL_BLOCK
    assert n_blk <= LANES and min(SEL_TOPK, n_blk) == SEL_TOPK
    cs = jnp.arange(ncp)[:, None] * CMP_STRIDE
    bs = jnp.arange(LANES)[None, :] * SEL_BLOCK
    ov = (jnp.clip(jnp.minimum(cs + CMP_BLOCK, bs + SEL_BLOCK) - jnp.maximum(cs, bs), 0, None)
          .astype(F32) / CMP_BLOCK).astype(BF16)

    o_nsa = _nsa(_alibi_slopes(NSA_HEADS), q_nsa, cmp_kv[0], cmp_kv[1], ks, vs, kw, vw,
                 gates, ov, tq, tk)
    o_nsa = o_nsa.transpose(0, 2, 1, 3).reshape(bsz, s, NSA_Q_W)

    def maps_first(a):
        return a.reshape(bsz, s, DIFF_HEADS, 2, dk).transpose(0, 2, 3, 1, 4)

    lam_init = 0.8 - 0.6 * math.exp(-0.3 * layer)
    o_diff = _diff(_alibi_slopes(DIFF_HEADS),
                   aug_q(maps_first(dq * scale), _alibi_slopes(DIFF_HEADS)),
                   aug_k(maps_first(dkk), False),
                   dv.reshape(bsz, s, DIFF_HEADS, DIFF_V_DIM).transpose(0, 2, 1, 3),
                   diff_lambda, diff_subln, tq, tk, lam_init)

    x1, hb, h_rows2d, logits = _outproj(o_nsa, o_diff, w_out.astype(BF16), x, g1, sc2, sh2,
                                        norm_g[1:2], norm_g[2:3], w_router, tm)

    oi, ow, cnt = _router(logits.reshape(n_tok, N_EXPERTS), router_bias, tm)
    idx_flat = oi[:, :TOP_K].reshape(-1)
    rank_flat = oi[:, TOP_K:2 * TOP_K].reshape(-1)
    w_flat = ow[:, :TOP_K].reshape(-1)
    counts = cnt[0].astype(jnp.int32)
    padded = (counts + EXPERT_CHUNK - 1) // EXPERT_CHUNK * EXPERT_CHUNK
    pend = jnp.cumsum(padded)
    pstart = (pend - padded).astype(jnp.int32)
    n_asg = n_tok * TOP_K
    n_chunks = (n_asg + N_EXPERTS * (EXPERT_CHUNK - 1) + EXPERT_CHUNK - 1) // EXPERT_CHUNK
    n_rows = n_chunks * EXPERT_CHUNK
    chunk_start = jnp.arange(n_chunks, dtype=jnp.int32) * EXPERT_CHUNK
    chunk_e = jnp.minimum(
        jnp.sum((pend[None, :] <= chunk_start[:, None]).astype(jnp.int32), axis=1), N_EXPERTS - 1)
    n_active = (pend[-1:] // EXPERT_CHUNK).astype(jnp.int32)
    first = jnp.concatenate([jnp.ones((1,), jnp.int32),
                             (chunk_e[1:] != chunk_e[:-1]).astype(jnp.int32)])
    slot = (jnp.cumsum(first) - 1) % 2
    e_ids = jnp.arange(N_EXPERTS, dtype=jnp.int32)
    from_here = lax.cummin(jnp.where(counts > 0, e_ids, N_EXPERTS), axis=0, reverse=True)
    after = jnp.concatenate([from_here[1:], jnp.full((1,), N_EXPERTS, jnp.int32)])
    next_e = after[chunk_e]
    next_e = jnp.where(next_e == N_EXPERTS, -1, next_e).astype(jnp.int32)

    row_tile = (d // LANES, LANES)
    xs = _dispatch(idx_flat, rank_flat, pstart, pend.astype(jnp.int32),
                   h_rows2d.reshape((n_tok,) + row_tile), n_rows, 128)
    ys = _experts(chunk_e, first, next_e, slot.astype(jnp.int32), n_active,
                  xs.reshape(n_rows * d // LANES, LANES), w_gate_up, w_down)
    routed = _combine(idx_flat, rank_flat, w_flat, pstart, ys.reshape((n_rows,) + row_tile),
                      n_tok, 128)
    return _final(hb, routed.reshape(n_tok * d // LANES, LANES), x1, g2, norm_g[3:4],
                  ws_gate_up.astype(BF16), ws_down.astype(BF16), tm)


def kernel(x, c, w_ada, b_ada, norm_g, w_in, w_cmp1, w_cmp2, pe_cmp, diff_lambda, diff_subln,
           w_out, w_router, router_bias, w_gate_up, w_down, ws_gate_up, ws_down):
    for layer in range(w_ada.shape[0]):
        x = _layer(x, c, w_ada[layer], b_ada[layer], norm_g[layer], w_in[layer], w_cmp1[layer],
                   w_cmp2[layer], pe_cmp[layer], diff_lambda[layer], diff_subln[layer],
                   w_out[layer], w_router[layer], router_bias[layer], w_gate_up[layer],
                   w_down[layer], ws_gate_up[layer], ws_down[layer], layer)
    return x
```

```python
import functools
import math

import jax
import jax.numpy as jnp
from jax import lax
from jax.experimental import pallas as pl
from jax.experimental.pallas import tpu as pltpu

HEAD_DIM = 64
NSA_HEADS = 8
NSA_KV_HEADS = 2
NSA_REP = NSA_HEADS // NSA_KV_HEADS
CMP_BLOCK = 32
CMP_STRIDE = 16
SEL_BLOCK = 64
SEL_TOPK = 16
WINDOW = 512
DIFF_HEADS = 4
DIFF_V_DIM = 2 * HEAD_DIM
N_EXPERTS = 256
TOP_K = 8
N_GROUPS = 8
TOPK_GROUPS = 4
EXPERT_DIM = 256
SHARED_DIM = 256
ROUTED_SCALE = 2.5
EXPERT_CHUNK = 128
RMS_EPS = 1e-6
NEG_INF = -1e30
BELOW_NEG_INF = -3e38
FORCE_BONUS = 1e4

NSA_Q_W = NSA_HEADS * HEAD_DIM
NSA_KV_W = NSA_KV_HEADS * HEAD_DIM
NSA_GATE_W = 3 * NSA_HEADS
DIFF_QK_W = DIFF_HEADS * 2 * HEAD_DIM
DIFF_V_W = DIFF_HEADS * DIFF_V_DIM
IN_SPLITS = (NSA_Q_W,) + (NSA_KV_W,) * 6 + (NSA_GATE_W, DIFF_QK_W, DIFF_QK_W, DIFF_V_W)
IN_WIDTH = sum(IN_SPLITS)

LANES = 128
SUBLANES = 8
VMEM_LIMIT = 56 * 1024 * 1024
KEY_TILE = 1024
DIFF_Q_TILE = 512

F32 = jnp.float32
BF16 = jnp.bfloat16


def _round_up(a, m):
    return (a + m - 1) // m * m


def _dot(a, b):
    return jnp.dot(a, b, preferred_element_type=F32)


def _dot_nt(a, b):
    return lax.dot_general(a, b, (((1,), (1,)), ((), ())), preferred_element_type=F32)


def _dot_split(a, b_bf16):
    hi = a.astype(BF16)
    lo = (a - hi.astype(F32)).astype(BF16)
    return _dot(hi, b_bf16) + _dot(lo, b_bf16)


def _dot_f32(a, b):
    a_hi = a.astype(BF16)
    a_lo = (a - a_hi.astype(F32)).astype(BF16)
    b_hi = b.astype(BF16)
    b_lo = (b - b_hi.astype(F32)).astype(BF16)
    return _dot(a_hi, b_hi) + (_dot(a_lo, b_hi) + _dot(a_hi, b_lo))


def _rms(x, g):
    return x * lax.rsqrt(jnp.mean(x * x, axis=-1, keepdims=True) + RMS_EPS) * g


def _params(*sem):
    return pltpu.CompilerParams(dimension_semantics=sem, vmem_limit_bytes=VMEM_LIMIT)


def _ada_kernel(c_ref, w_ref, b_ref, o_ref):
    c = c_ref[...]
    h = c * jax.nn.sigmoid(c)
    o_ref[...] = _dot_f32(h, w_ref[...]) + b_ref[...]


def _ada(c, w, b):
    bsz, d = c.shape
    n = w.shape[1]
    tn = n // 4
    cp = jnp.zeros((SUBLANES, d), F32).at[:bsz].set(c)
    out = pl.pallas_call(
        _ada_kernel,
        grid=(n // tn,),
        in_specs=[pl.BlockSpec((SUBLANES, d), lambda j: (0, 0)),
                  pl.BlockSpec((d, tn), lambda j: (0, j)),
                  pl.BlockSpec((1, tn), lambda j: (0, j))],
        out_specs=pl.BlockSpec((SUBLANES, tn), lambda j: (0, j)),
        out_shape=jax.ShapeDtypeStruct((SUBLANES, n), F32),
        compiler_params=_params("arbitrary"),
    )(cp, w, b.reshape(1, n))
    return out[:bsz]


def _inproj_kernel(x_ref, sc_ref, sh_ref, g_ref, w_ref, o_ref):
    h = _rms(x_ref[0], g_ref[...]) * (1.0 + sc_ref[0]) + sh_ref[0]
    o_ref[0] = _dot(h.astype(BF16), w_ref[...]).astype(o_ref.dtype)


def _inproj(x, sc, sh, g, w_bf16, tm):
    bsz, s, d = x.shape
    n = w_bf16.shape[1]
    return pl.pallas_call(
        _inproj_kernel,
        grid=(bsz, s // tm),
        in_specs=[pl.BlockSpec((1, tm, d), lambda b, i: (b, i, 0)),
                  pl.BlockSpec((1, 1, d), lambda b, i: (b, 0, 0)),
                  pl.BlockSpec((1, 1, d), lambda b, i: (b, 0, 0)),
                  pl.BlockSpec((1, d), lambda b, i: (0, 0)),
                  pl.BlockSpec((d, n), lambda b, i: (0, 0))],
        out_specs=pl.BlockSpec((1, tm, n), lambda b, i: (b, i, 0)),
        out_shape=jax.ShapeDtypeStruct((bsz, s, n), BF16),
        compiler_params=_params("parallel", "parallel"),
    )(x, sc, sh, g, w_bf16)


def _cmp_kernel(ch_ref, w1_ref, w2_ref, pe_ref, o_ref):
    ch = ch_ref[0, 0, 0]
    half = ch.shape[1]
    a = _dot(ch, w1_ref[0, 0])
    bm = _dot(ch, w1_ref[0, 1])
    n = bm.shape[0]
    bm_next = pltpu.roll(bm, shift=n - 1, axis=0)
    pe = pe_ref[0]
    pt = _dot(pe[:, :half], w1_ref[0, 0]) + _dot(pe[:, half:], w1_ref[0, 1])
    hmid = jax.nn.gelu(a + bm_next + pt[0:1])
    o_ref[0, 0, 0] = _dot(hmid.astype(BF16), w2_ref[0]).astype(o_ref.dtype)


def _compress(chunks, w1, w2, pe8):
    two, bsz, g, ncp, cw = chunks.shape
    dk = w2.shape[-1]
    return pl.pallas_call(
        _cmp_kernel,
        grid=(two, bsz, g),
        in_specs=[pl.BlockSpec((1, 1, 1, ncp, cw), lambda a, b, c: (a, b, c, 0, 0)),
                  pl.BlockSpec((1, 2, cw, dk), lambda a, b, c: (a, 0, 0, 0)),
                  pl.BlockSpec((1, dk, dk), lambda a, b, c: (a, 0, 0)),
                  pl.BlockSpec((1, SUBLANES, 2 * cw), lambda a, b, c: (a, 0, 0))],
        out_specs=pl.BlockSpec((1, 1, 1, ncp, dk), lambda a, b, c: (a, b, c, 0, 0)),
        out_shape=jax.ShapeDtypeStruct((two, bsz, g, ncp, dk), BF16),
        compiler_params=_params("parallel", "parallel", "parallel"),
    )(chunks, w1, w2, pe8)


AUG_W = LANES
POS_SPLIT = 64


def _flash_tile(s, delta, carry, v):
    m, l, acc = carry
    m_new = jnp.maximum(m, jnp.max(s, axis=-1, keepdims=True) + delta)
    pr = jnp.exp(s - (m_new - delta))
    alpha = jnp.exp(m - m_new)
    return (m_new, alpha * l + jnp.sum(pr, axis=-1, keepdims=True),
            alpha * acc + _dot(pr.astype(BF16), v))


def _nsa_kernel(slopes_ref, q_ref, kc_ref, vc_ref, ks_ref, vs_ref, kw_ref, vw_ref,
                g_ref, ov_ref, o_ref, bw_ref, m_ref, l_ref, acc_ref, *, tq, tk):
    g = pl.program_id(1)
    i = pl.program_id(2)
    rep = NSA_REP
    rows = rep * tq
    s0 = i * tq
    qa = q_ref[0].reshape(rows, AUG_W)
    q4 = qa[:, :HEAD_DIM]
    t = s0 + lax.broadcasted_iota(jnp.int32, (tq, 1), 0)
    t4 = jnp.concatenate([t] * rep, axis=0)
    slope4 = jnp.concatenate(
        [jnp.full((tq, 1), slopes_ref[g * rep + r], F32) for r in range(rep)], axis=0)
    wl = WINDOW + tq

    @pl.when(i <= WINDOW // tq)
    def _():
        row = lax.broadcasted_iota(jnp.int32, (tq, 1), 0)
        dw = (jnp.concatenate([row] * rep, axis=0)
              + (jnp.minimum(s0, WINDOW) - lax.broadcasted_iota(jnp.int32, (1, wl), 1)))
        bw_ref[...] = jnp.where((dw >= 0) & (dw < WINDOW), -slope4 * dw.astype(F32), NEG_INF)

    kc = kc_ref[0, 0]
    ncp = kc.shape[0]
    sc = _dot_nt(q4, kc)
    cend = lax.broadcasted_iota(jnp.int32, (1, ncp), 1) * CMP_STRIDE + (CMP_BLOCK - 1)
    dist = t4 - cend
    vis = dist >= 0
    sc = jnp.where(vis, sc - slope4 * dist.astype(F32), NEG_INF)
    mx = jnp.max(sc, axis=-1, keepdims=True)
    e = jnp.where(vis, jnp.exp(sc - mx), 0.0)
    p = e / jnp.maximum(jnp.sum(e, axis=-1, keepdims=True), 1e-30)
    o_c = _dot(p.astype(BF16), vc_ref[0, 0])

    psum = p[0:tq]
    for r in range(1, rep):
        psum = psum + p[r * tq:(r + 1) * tq]
    imp = _dot_split(psum, ov_ref[...])
    jf = lax.broadcasted_iota(jnp.int32, (tq, LANES), 1)
    qb = t // SEL_BLOCK
    forced = (jf == 0) | (jf == qb) | (jf == qb - 1)
    v = jnp.where(jf <= qb, imp + jnp.where(forced, FORCE_BONUS, 0.0), NEG_INF)
    jff = jf.astype(F32)
    selb = jnp.full((tq, LANES), NEG_INF, F32)
    for _ in range(SEL_TOPK):
        vmax = jnp.max(v, axis=-1, keepdims=True)
        first = jnp.min(jnp.where(v == vmax, jff, float(LANES)), axis=-1, keepdims=True)
        pick = jff == first
        selb = jnp.where(pick, 0.0, selb)
        v = jnp.where(pick, BELOW_NEG_INF, v)
    chosen = jnp.max(jnp.where(selb == 0.0, 1.0, 0.0), axis=0, keepdims=True)
    selb = selb.astype(BF16)

    qs = jnp.concatenate([qa, jnp.concatenate([selb] * rep, axis=0)], axis=1)

    def sel_tile(kt, masked):
        k0 = pl.multiple_of(kt * tk, tk)
        s = _dot_nt(qs, ks_ref[0, 0, pl.ds(k0, tk), :])
        if masked:
            s = jnp.where(t4 >= k0 + lax.broadcasted_iota(jnp.int32, (1, tk), 1), s, NEG_INF)
        delta = slope4 * (k0 - s0).astype(F32)
        m_ref[...], l_ref[...], acc_ref[...] = _flash_tile(
            s, delta, (m_ref[...], l_ref[...], acc_ref[...]), vs_ref[0, 0, pl.ds(k0, tk), :])

    m_ref[...] = jnp.full((rows, 1), NEG_INF, F32)
    l_ref[...] = jnp.zeros((rows, 1), F32)
    acc_ref[...] = jnp.zeros((rows, HEAD_DIM), F32)
    blk_lane = lax.broadcasted_iota(jnp.int32, (1, LANES), 1)
    blk_per_tile = tk // SEL_BLOCK

    def sel_full_tile(kt, carry):
        in_tile = (blk_lane >= kt * blk_per_tile) & (blk_lane < (kt + 1) * blk_per_tile)

        @pl.when(jnp.max(jnp.where(in_tile, chosen, 0.0)) > 0.0)
        def _():
            sel_tile(kt, False)

        return carry

    n_full = s0 // tk
    lax.fori_loop(0, n_full, sel_full_tile, 0)
    sel_tile(n_full, True)
    o_s = acc_ref[...] / l_ref[...]

    ws = pl.multiple_of(jnp.maximum(s0 - WINDOW, 0), tq)
    sw = _dot_nt(q4, kw_ref[0, 0, pl.ds(ws, wl), :]) + bw_ref[...]
    ew = jnp.exp(sw - jnp.max(sw, axis=-1, keepdims=True))
    o_w = _dot(ew.astype(BF16), vw_ref[0, 0, pl.ds(ws, wl), :]) / jnp.sum(ew, axis=-1, keepdims=True)

    gate = jax.nn.sigmoid(g_ref[0, 0].astype(F32))
    for r in range(rep):
        rr = slice(r * tq, (r + 1) * tq)
        o = (gate[:, 3 * r:3 * r + 1] * o_c[rr] + gate[:, 3 * r + 1:3 * r + 2] * o_s[rr]
             + gate[:, 3 * r + 2:3 * r + 3] * o_w[rr])
        o_ref[0, r] = o.astype(o_ref.dtype)


def _nsa(slopes, q, kc, vc, ks, vs, kw, vw, gates, ov, tq, tk):
    bsz, h, s, _ = q.shape
    g = NSA_KV_HEADS
    rep = NSA_REP
    dk = HEAD_DIM
    assert s % tk == 0 and tk % tq == 0 and tq % SEL_BLOCK == 0 and tk // POS_SPLIT <= 256
    assert s >= WINDOW + tq and WINDOW % tq == 0

    def resident(a):
        return pl.BlockSpec((1, 1) + a.shape[2:], lambda b, c, i: (b, c, 0, 0))

    return pl.pallas_call(
        functools.partial(_nsa_kernel, tq=tq, tk=tk),
        grid=(bsz, g, s // tq),
        in_specs=[pl.BlockSpec(memory_space=pltpu.SMEM),
                  pl.BlockSpec((1, rep, tq, AUG_W), lambda b, c, i: (b, c, i, 0)),
                  resident(kc), resident(vc), resident(ks), resident(vs), resident(kw), resident(vw),
                  pl.BlockSpec((1, 1, tq, 3 * rep), lambda b, c, i: (b, c, i, 0)),
                  pl.BlockSpec(ov.shape, lambda b, c, i: (0, 0))],
        out_specs=pl.BlockSpec((1, rep, tq, dk), lambda b, c, i: (b, c, i, 0)),
        out_shape=jax.ShapeDtypeStruct((bsz, h, s, dk), BF16),
        scratch_shapes=[pltpu.VMEM((rep * tq, WINDOW + tq), F32),
                        pltpu.VMEM((rep * tq, 1), F32), pltpu.VMEM((rep * tq, 1), F32),
                        pltpu.VMEM((rep * tq, dk), F32)],
        compiler_params=_params("parallel", "parallel", "arbitrary"),
    )(slopes, q, kc, vc, ks, vs, kw, vw, gates, ov)


def _diff_kernel(slopes_ref, q_ref, k_ref, v_ref, lam_ref, g_ref, o_ref, *, tq, tk, lam_init):
    h = pl.program_id(1)
    i = pl.program_id(2)
    s0 = i * tq
    slope = slopes_ref[h]
    t = s0 + lax.broadcasted_iota(jnp.int32, (tq, 1), 0)

    def tile(kt, carry, masked):
        k0 = pl.multiple_of(kt * tk, tk)
        delta = slope * (k0 - s0).astype(F32)
        v = v_ref[0, 0, pl.ds(k0, tk), :]
        new = []
        for mi in range(2):
            s = _dot_nt(q_ref[0, 0, mi], k_ref[0, 0, mi, pl.ds(k0, tk), :])
            if masked:
                s = jnp.where(t >= k0 + lax.broadcasted_iota(jnp.int32, (1, tk), 1), s, NEG_INF)
            new.append(_flash_tile(s, delta, carry[mi], v))
        return tuple(new)

    one = (jnp.full((tq, 1), NEG_INF, F32), jnp.zeros((tq, 1), F32),
           jnp.zeros((tq, DIFF_V_DIM), F32))
    n_full = s0 // tk
    carry = lax.fori_loop(0, n_full, functools.partial(tile, masked=False), (one, one))
    (_, l0, acc0), (_, l1, acc1) = tile(n_full, carry, True)
    lp = lam_ref[...]
    lam = (jnp.exp(jnp.sum(lp[0:1] * lp[1:2], axis=-1, keepdims=True))
           - jnp.exp(jnp.sum(lp[2:3] * lp[3:4], axis=-1, keepdims=True)) + lam_init)
    o = acc0 / l0 - lam * (acc1 / l1)
    o_ref[0] = (_rms(o, g_ref[...]) * (1.0 - lam_init)).astype(o_ref.dtype)


def _diff(slopes, q, k, v, lam_params, subln_g, tq, tk, lam_init):
    bsz, h, _, s, _ = q.shape
    dv = v.shape[-1]
    assert s % tk == 0 and tk % tq == 0 and tk // POS_SPLIT <= 256
    return pl.pallas_call(
        functools.partial(_diff_kernel, tq=tq, tk=tk, lam_init=lam_init),
        grid=(bsz, h, s // tq),
        in_specs=[pl.BlockSpec(memory_space=pltpu.SMEM),
                  pl.BlockSpec((1, 1, 2, tq, AUG_W), lambda b, c, i: (b, c, 0, i, 0)),
                  pl.BlockSpec((1, 1, 2, s, AUG_W), lambda b, c, i: (b, c, 0, 0, 0)),
                  pl.BlockSpec((1, 1, s, dv), lambda b, c, i: (b, c, 0, 0)),
                  pl.BlockSpec(lam_params.shape, lambda b, c, i: (0, 0)),
                  pl.BlockSpec((1, dv), lambda b, c, i: (0, 0))],
        out_specs=pl.BlockSpec((1, tq, dv), lambda b, c, i: (b, i, c)),
        out_shape=jax.ShapeDtypeStruct((bsz, s, h * dv), BF16),
        compiler_params=_params("parallel", "parallel", "arbitrary"),
    )(slopes, q, k, v, lam_params, subln_g.reshape(1, dv))


def _outproj_kernel(a_ref, b_ref, wo_ref, x_ref, g1_ref, sc_ref, sh_ref, ng1_ref, ng2_ref,
                    wr_ref, x1_ref, hb_ref, hrow_ref, lg_ref):
    half = a_ref.shape[-1]
    tm = a_ref.shape[1]
    y = _dot(a_ref[0], wo_ref[0:half, :]) + _dot(b_ref[0], wo_ref[half:, :])
    x1 = x_ref[0] + g1_ref[0] * _rms(y, ng1_ref[...])
    x1_ref[0] = x1
    h = _rms(x1, ng2_ref[...]) * (1.0 + sc_ref[0]) + sh_ref[0]
    hb_ref[0] = h.astype(BF16)
    for s in range(h.shape[1] // LANES):
        hrow_ref[pl.ds(s, tm, stride=SUBLANES), :] = h[:, s * LANES:(s + 1) * LANES]
    lg_ref[0] = _dot_f32(h, wr_ref[...])


def _outproj(a, b, wo, x, g1, sc2, sh2, ng1, ng2, wr, tm):
    bsz, s, d = x.shape
    half = a.shape[-1]
    ne = wr.shape[1]
    nt = s // tm
    vec = pl.BlockSpec((1, 1, d), lambda bb, i: (bb, 0, 0))
    row = pl.BlockSpec((1, d), lambda bb, i: (0, 0))
    tile = pl.BlockSpec((1, tm, d), lambda bb, i: (bb, i, 0))
    return pl.pallas_call(
        _outproj_kernel,
        grid=(bsz, nt),
        in_specs=[pl.BlockSpec((1, tm, half), lambda bb, i: (bb, i, 0)),
                  pl.BlockSpec((1, tm, half), lambda bb, i: (bb, i, 0)),
                  pl.BlockSpec((d, d), lambda bb, i: (0, 0)),
                  tile, vec, vec, vec, row, row,
                  pl.BlockSpec((d, ne), lambda bb, i: (0, 0))],
        out_specs=[tile, tile,
                   pl.BlockSpec((tm * d // LANES, LANES), lambda bb, i: (bb * nt + i, 0)),
                   pl.BlockSpec((1, tm, ne), lambda bb, i: (bb, i, 0))],
        out_shape=[jax.ShapeDtypeStruct((bsz, s, d), F32),
                   jax.ShapeDtypeStruct((bsz, s, d), BF16),
                   jax.ShapeDtypeStruct((bsz * s * d // LANES, LANES), F32),
                   jax.ShapeDtypeStruct((bsz, s, ne), F32)],
        compiler_params=_params("parallel", "parallel"),
    )(a, b, wo, x, g1, sc2, sh2, ng1, ng2, wr)


def _router_kernel(lg_ref, bias_ref, tri_ref, oi_ref, ow_ref, cnt_ref, carry_ref):
    @pl.when(pl.program_id(0) == 0)
    def _():
        carry_ref[...] = jnp.zeros_like(carry_ref)

    tm, ne = lg_ref.shape
    per_grp = ne // N_GROUPS
    aff = jax.nn.sigmoid(lg_ref[...])
    choice = aff + bias_ref[...]
    lane = lax.broadcasted_iota(jnp.int32, (tm, ne), 1)
    lanef = lane.astype(F32)
    grp = lane // per_grp

    def row_max(a):
        return jnp.max(a, axis=-1, keepdims=True)

    def first_lane(eq):
        return jnp.min(jnp.where(eq, lanef, float(ne)), axis=-1, keepdims=True)

    gscore = []
    for gi in range(N_GROUPS):
        mg = jnp.where(grp == gi, choice, BELOW_NEG_INF)
        m1 = row_max(mg)
        mg = jnp.where(lanef == first_lane(mg == m1), BELOW_NEG_INF, mg)
        gscore.append(m1 + row_max(mg))
    keep = jnp.zeros((tm, ne), F32)
    for gi in range(N_GROUPS):
        beaten = jnp.zeros((tm, 1), F32)
        for gj in range(N_GROUPS):
            if gj == gi:
                continue
            wins = gscore[gj] >= gscore[gi] if gj < gi else gscore[gj] > gscore[gi]
            beaten = beaten + jnp.where(wins, 1.0, 0.0)
        keep = jnp.where(grp == gi, jnp.where(beaten < TOPK_GROUPS, 1.0, 0.0), keep)
    ch = jnp.where(keep > 0.5, choice, NEG_INF)

    idx, wts = [], []
    onehot = jnp.zeros((tm, ne), F32)
    for _ in range(TOP_K):
        first = first_lane(ch == row_max(ch))
        pick = lanef == first
        wts.append(jnp.sum(jnp.where(pick, aff, 0.0), axis=-1, keepdims=True))
        ch = jnp.where(pick, BELOW_NEG_INF, ch)
        onehot = jnp.where(pick, 1.0, onehot)
        idx.append(first)
    wsum = wts[0]
    for w in wts[1:]:
        wsum = wsum + w
    before = carry_ref[...] + _dot(tri_ref[...], onehot.astype(BF16))
    carry_ref[...] = carry_ref[...] + jnp.sum(onehot, axis=0, keepdims=True)
    cnt_ref[...] = jnp.broadcast_to(carry_ref[...], cnt_ref.shape)

    l128 = lax.broadcasted_iota(jnp.int32, (tm, LANES), 1)
    oi = jnp.zeros((tm, LANES), jnp.int32)
    ow = jnp.zeros((tm, LANES), F32)
    for k in range(TOP_K):
        rank = jnp.sum(jnp.where(lanef == idx[k], before, 0.0), axis=-1, keepdims=True)
        oi = jnp.where(l128 == k, idx[k].astype(jnp.int32), oi)
        oi = jnp.where(l128 == TOP_K + k, rank.astype(jnp.int32), oi)
        ow = jnp.where(l128 == k, wts[k] / wsum * ROUTED_SCALE, ow)
    oi_ref[...] = oi
    ow_ref[...] = ow


def _router(logits, bias, tm):
    n_tok, ne = logits.shape
    tri = (jnp.arange(tm)[:, None] > jnp.arange(tm)[None, :]).astype(BF16)
    return pl.pallas_call(
        _router_kernel,
        grid=(n_tok // tm,),
        in_specs=[pl.BlockSpec((tm, ne), lambda i: (i, 0)),
                  pl.BlockSpec((1, ne), lambda i: (0, 0)),
                  pl.BlockSpec((tm, tm), lambda i: (0, 0))],
        out_specs=[pl.BlockSpec((tm, LANES), lambda i: (i, 0)),
                   pl.BlockSpec((tm, LANES), lambda i: (i, 0)),
                   pl.BlockSpec((SUBLANES, ne), lambda i: (0, 0))],
        out_shape=[jax.ShapeDtypeStruct((n_tok, LANES), jnp.int32),
                   jax.ShapeDtypeStruct((n_tok, LANES), F32),
                   jax.ShapeDtypeStruct((SUBLANES, ne), F32)],
        scratch_shapes=[pltpu.VMEM((1, ne), F32)],
        compiler_params=_params("arbitrary"),
    )(logits, bias.reshape(1, ne), tri)


def _row_copy(src, src_row, dst, dst_row, sem):
    return pltpu.make_async_copy(src.at[src_row], dst.at[dst_row], sem)


def _chunk_fill(zbuf, xs_ref, start, sem):
    return pltpu.make_async_copy(zbuf, xs_ref.at[pl.ds(start, EXPERT_CHUNK)], sem)


DISPATCH_SLOTS = 3


def _dispatch_kernel(idx_ref, rank_ref, pstart_ref, pend_ref, h_hbm, xs_ref, zbuf, stage, sem, zsem,
                     lsem, *, th):
    step = pl.program_id(0)
    n_steps = pl.num_programs(0)

    @pl.when(step == 0)
    def _():
        zbuf[...] = jnp.zeros_like(zbuf)

        def fill(e, carry):
            @pl.when(pend_ref[e] > pstart_ref[e])
            def _():
                _chunk_fill(zbuf, xs_ref, pend_ref[e] - EXPERT_CHUNK, zsem).start()
            return carry

        lax.fori_loop(0, N_EXPERTS, fill, 0)

        def fill_done(e, carry):
            @pl.when(pend_ref[e] > pstart_ref[e])
            def _():
                _chunk_fill(zbuf, xs_ref, 0, zsem).wait()
            return carry

        lax.fori_loop(0, N_EXPERTS, fill_done, 0)

        first_unused = pend_ref[N_EXPERTS - 1] // EXPERT_CHUNK
        n_chunks = xs_ref.shape[0] // EXPERT_CHUNK

        def tail(ci, carry):
            _chunk_fill(zbuf, xs_ref, ci * EXPERT_CHUNK, zsem).start()
            return carry

        lax.fori_loop(first_unused, n_chunks, tail, 0)

        def tail_done(ci, carry):
            _chunk_fill(zbuf, xs_ref, 0, zsem).wait()
            return carry

        lax.fori_loop(first_unused, n_chunks, tail_done, 0)

    def load(s, slot):
        return pltpu.make_async_copy(h_hbm.at[pl.ds(s * th, th)], stage.at[slot], lsem.at[slot])

    @pl.when(step == 0)
    def _():
        load(0, 0).start()

    @pl.when(step + 1 < n_steps)
    def _():
        load(step + 1, (step + 1) % DISPATCH_SLOTS).start()

    slot = step % DISPATCH_SLOTS
    load(step, slot).wait()
    src = stage.at[slot]

    def issue(tl, carry):
        for k in range(TOP_K):
            a = tl * TOP_K + k
            dst = pstart_ref[idx_ref[a]] + rank_ref[a]
            _row_copy(src, tl, xs_ref, dst, sem.at[step % 2]).start()
        return carry

    lax.fori_loop(0, th, issue, 0)

    def drain(parity):
        def body(tl, carry):
            for _ in range(TOP_K):
                _row_copy(src, 0, xs_ref, 0, sem.at[parity]).wait()
            return carry

        lax.fori_loop(0, th, body, 0)

    @pl.when(step > 0)
    def _():
        drain((step - 1) % 2)

    @pl.when(step == n_steps - 1)
    def _():
        drain(step % 2)


def _dispatch(idx_flat, rank_flat, pstart, pend, h_rows, n_rows, th):
    n_tok = h_rows.shape[0]
    row_shape = h_rows.shape[1:]
    smem_blk = pl.BlockSpec((th * TOP_K,), lambda i: (i,), memory_space=pltpu.SMEM)
    return pl.pallas_call(
        functools.partial(_dispatch_kernel, th=th),
        grid=(n_tok // th,),
        in_specs=[smem_blk, smem_blk,
                  pl.BlockSpec(memory_space=pltpu.SMEM),
                  pl.BlockSpec(memory_space=pltpu.SMEM),
                  pl.BlockSpec(memory_space=pl.ANY)],
        out_specs=pl.BlockSpec(memory_space=pl.ANY),
        out_shape=jax.ShapeDtypeStruct((n_rows,) + row_shape, h_rows.dtype),
        scratch_shapes=[pltpu.VMEM((EXPERT_CHUNK,) + row_shape, h_rows.dtype),
                        pltpu.VMEM((DISPATCH_SLOTS, th) + row_shape, h_rows.dtype),
                        pltpu.SemaphoreType.DMA((2,)), pltpu.SemaphoreType.DMA(()),
                        pltpu.SemaphoreType.DMA((DISPATCH_SLOTS,))],
        compiler_params=_params("arbitrary"),
    )(idx_flat, rank_flat, pstart, pend, h_rows)


def _weight_fetch(wgu_hbm, wdn_hbm, gu_buf, dn_buf, sems, expert, slot):
    return (pltpu.make_async_copy(wgu_hbm.at[expert], gu_buf.at[slot], sems.at[slot, 0]),
            pltpu.make_async_copy(wdn_hbm.at[expert], dn_buf.at[slot], sems.at[slot, 1]))


def _experts_kernel(ce_ref, first_ref, next_ref, slot_ref, na_ref, x_ref, wgu_hbm, wdn_hbm, o_ref,
                    gu_buf, dn_buf, gu_bf, dn_bf, sems):
    i = pl.program_id(0)
    ch = EXPERT_CHUNK
    nsl = x_ref.shape[0] // ch
    active = i < na_ref[0]
    fetch = functools.partial(_weight_fetch, wgu_hbm, wdn_hbm, gu_buf, dn_buf, sems)

    @pl.when(i == 0)
    def _():
        for cp in fetch(ce_ref[0], 0):
            cp.start()

    @pl.when(active & (first_ref[i] == 1))
    def _():
        slot = slot_ref[i]
        for cp in fetch(ce_ref[i], slot):
            cp.wait()

        @pl.when(next_ref[i] >= 0)
        def _():
            for cp in fetch(next_ref[i], 1 - slot):
                cp.start()

        gu_bf[...] = gu_buf[slot].astype(BF16)
        dn_bf[...] = dn_buf[slot].astype(BF16)

    @pl.when(active)
    def _():
        x = jnp.concatenate(
            [x_ref[pl.ds(s, ch, stride=SUBLANES), :] for s in range(nsl)], axis=1).astype(BF16)
        au = _dot(x, gu_bf[...])
        a = au[:, :EXPERT_DIM]
        u = au[:, EXPERT_DIM:]
        hmid = (a * jax.nn.sigmoid(a) * u).astype(BF16)
        y = _dot(hmid, dn_bf[...])
        for s in range(nsl):
            o_ref[pl.ds(s, ch, stride=SUBLANES), :] = y[:, s * LANES:(s + 1) * LANES]

    @pl.when(jnp.logical_not(active))
    def _():
        o_ref[...] = jnp.zeros_like(o_ref)


def _experts(chunk_e, first, next_e, slot, n_active, xs2d, w_gu, w_dn):
    n_chunks = chunk_e.shape[0]
    _, d, two_e = w_gu.shape
    blk_rows = EXPERT_CHUNK * d // LANES

    def x_map(i, ce, fs, nx, sl, na):
        return (jnp.minimum(i, na[0] - 1), 0)

    grid_spec = pltpu.PrefetchScalarGridSpec(
        num_scalar_prefetch=5,
        grid=(n_chunks,),
        in_specs=[pl.BlockSpec((blk_rows, LANES), x_map),
                  pl.BlockSpec(memory_space=pl.ANY),
                  pl.BlockSpec(memory_space=pl.ANY)],
        out_specs=pl.BlockSpec((blk_rows, LANES), lambda i, ce, fs, nx, sl, na: (i, 0)),
        scratch_shapes=[pltpu.VMEM((2, d, two_e), F32), pltpu.VMEM((2, two_e // 2, d), F32),
                        pltpu.VMEM((d, two_e), BF16), pltpu.VMEM((two_e // 2, d), BF16),
                        pltpu.SemaphoreType.DMA((2, 2))],
    )
    return pl.pallas_call(
        _experts_kernel,
        grid_spec=grid_spec,
        out_shape=jax.ShapeDtypeStruct(xs2d.shape, F32),
        compiler_params=_params("arbitrary"),
    )(chunk_e, first, next_e, slot, n_active, xs2d, w_gu, w_dn)


def _combine_kernel(idx_ref, rank_ref, idx_next_ref, rank_next_ref, w_ref, pstart_ref, y_ref, o_ref,
                    buf, sem, *, tj):
    step = pl.program_id(0)
    n_steps = pl.num_programs(0)
    cur = step % 2

    def gather(idx, rank, slot):
        def issue(tl, carry):
            for k in range(TOP_K):
                a = tl * TOP_K + k
                src = pstart_ref[idx[a]] + rank[a]
                _row_copy(y_ref, src, buf.at[slot], a, sem.at[slot]).start()
            return carry

        lax.fori_loop(0, tj, issue, 0)

    @pl.when(step == 0)
    def _():
        gather(idx_ref, rank_ref, 0)

    @pl.when(step + 1 < n_steps)
    def _():
        gather(idx_next_ref, rank_next_ref, 1 - cur)

    rows = buf.at[cur]

    def drain(tl, carry):
        for _ in range(TOP_K):
            _row_copy(y_ref, 0, rows, 0, sem.at[cur]).wait()
        return carry

    lax.fori_loop(0, tj, drain, 0)

    def reduce(tl, carry):
        acc = w_ref[tl * TOP_K] * rows[tl * TOP_K]
        for k in range(1, TOP_K):
            acc = acc + w_ref[tl * TOP_K + k] * rows[tl * TOP_K + k]
        o_ref[tl] = acc
        return carry

    lax.fori_loop(0, tj, reduce, 0)


def _combine(idx_flat, rank_flat, w_flat, pstart, y_rows, n_tok, tj):
    row_shape = y_rows.shape[1:]
    n_steps = n_tok // tj
    smem_blk = pl.BlockSpec((tj * TOP_K,), lambda i: (i,), memory_space=pltpu.SMEM)
    smem_next = pl.BlockSpec((tj * TOP_K,), lambda i: (jnp.minimum(i + 1, n_steps - 1),),
                             memory_space=pltpu.SMEM)
    return pl.pallas_call(
        functools.partial(_combine_kernel, tj=tj),
        grid=(n_steps,),
        in_specs=[smem_blk, smem_blk, smem_next, smem_next, smem_blk,
                  pl.BlockSpec(memory_space=pltpu.SMEM),
                  pl.BlockSpec(memory_space=pl.ANY)],
        out_specs=pl.BlockSpec((tj,) + row_shape, lambda i: (i, 0, 0)),
        out_shape=jax.ShapeDtypeStruct((n_tok,) + row_shape, F32),
        scratch_shapes=[pltpu.VMEM((2, tj * TOP_K) + row_shape, F32),
                        pltpu.SemaphoreType.DMA((2,))],
        compiler_params=_params("arbitrary"),
    )(idx_flat, rank_flat, idx_flat, rank_flat, w_flat, pstart, y_rows)


def _final_kernel(hb_ref, r_ref, x1_ref, g2_ref, ng_ref, wgu_ref, wdn_ref, o_ref):
    tm, d = hb_ref.shape[1:]
    au = _dot(hb_ref[0], wgu_ref[...])
    a = au[:, :SHARED_DIM]
    u = au[:, SHARED_DIM:]
    shared = _dot((a * jax.nn.sigmoid(a) * u).astype(BF16), wdn_ref[...])
    routed = jnp.concatenate(
        [r_ref[pl.ds(s, tm, stride=SUBLANES), :] for s in range(d // LANES)], axis=1)
    o_ref[0] = x1_ref[0] + g2_ref[0] * _rms(routed + shared, ng_ref[...])


def _final(hb, routed2d, x1, g2, ng, wgu, wdn, tm):
    bsz, s, d = x1.shape
    nt = s // tm
    tile = pl.BlockSpec((1, tm, d), lambda bb, i: (bb, i, 0))
    return pl.pallas_call(
        _final_kernel,
        grid=(bsz, nt),
        in_specs=[tile,
                  pl.BlockSpec((tm * d // LANES, LANES), lambda bb, i: (bb * nt + i, 0)),
                  tile,
                  pl.BlockSpec((1, 1, d), lambda bb, i: (bb, 0, 0)),
                  pl.BlockSpec((1, d), lambda bb, i: (0, 0)),
                  pl.BlockSpec(wgu.shape, lambda bb, i: (0, 0)),
                  pl.BlockSpec(wdn.shape, lambda bb, i: (0, 0))],
        out_specs=tile,
        out_shape=jax.ShapeDtypeStruct((bsz, s, d), F32),
        compiler_params=_params("parallel", "parallel"),
    )(hb, routed2d, x1, g2, ng, wgu, wdn)


def _alibi_slopes(n):
    return (2.0 ** (-8.0 * jnp.arange(1, n + 1, dtype=F32) / n)).astype(F32)


def _layer(x, c, w_ada, b_ada, norm_g, w_in, w_cmp1, w_cmp2, pe_cmp, diff_lambda, diff_subln,
           w_out, w_router, router_bias, w_gate_up, w_down, ws_gate_up, ws_down, layer):
    bsz, s, d = x.shape
    n_tok = bsz * s
    tm = 256
    tq = 256
    tk = min(KEY_TILE, s // 2)
    dk = HEAD_DIM
    g = NSA_KV_HEADS

    mod = _ada(c, w_ada, b_ada)
    sh1, sc1, g1, sh2, sc2, g2 = [m[:, None, :] for m in jnp.split(mod, 6, axis=-1)]

    n_pad = _round_up(IN_WIDTH, LANES)
    w_in_p = jnp.zeros((d, n_pad), BF16).at[:, :IN_WIDTH].set(w_in.astype(BF16))
    proj = _inproj(x, sc1, sh1, norm_g[0:1], w_in_p, tm)

    offs = [0]
    for wdt in IN_SPLITS:
        offs.append(offs[-1] + wdt)
    pieces = [proj[..., offs[j]:offs[j + 1]] for j in range(len(IN_SPLITS))]
    nq, kcm, vcm, ksel, vsel, kwin, vwin, ngate, dq, dkk, dv = pieces
    scale = dk ** -0.5

    def heads_first(a, nh):
        return a.reshape(bsz, s, nh, dk).transpose(0, 2, 1, 3)

    def aug_q(qh, slopes):
        shape = [1] * qh.ndim
        shape[1] = slopes.shape[0]
        sl = slopes.reshape(shape)
        feat = [jnp.broadcast_to(f, qh.shape[:-1] + (1,)).astype(BF16) for f in (sl * POS_SPLIT, sl)]
        pad = jnp.zeros(qh.shape[:-1] + (AUG_W - dk - len(feat),), BF16)
        return jnp.concatenate([qh] + feat + [pad], axis=-1)

    def aug_k(kh, block_onehot):
        col = jnp.arange(s, dtype=jnp.int32) % tk
        feat = [jnp.broadcast_to(f.astype(BF16)[:, None], kh.shape[:-1] + (1,))
                for f in (col // POS_SPLIT, col % POS_SPLIT)]
        pad = jnp.zeros(kh.shape[:-1] + (AUG_W - dk - len(feat),), BF16)
        parts = [kh] + feat + [pad]
        if block_onehot:
            oh = (jnp.arange(s)[:, None] // SEL_BLOCK == jnp.arange(LANES)[None, :]).astype(BF16)
            parts.append(jnp.broadcast_to(oh, kh.shape[:-1] + (LANES,)))
        return jnp.concatenate(parts, axis=-1)

    q_nsa = aug_q(heads_first(nq * scale, NSA_HEADS), _alibi_slopes(NSA_HEADS))
    ks = aug_k(heads_first(ksel, g), True)
    vs, kw, vw = [heads_first(a, g) for a in (vsel, kwin, vwin)]
    gates = ngate.reshape(bsz, s, g, 3 * NSA_REP).transpose(0, 2, 1, 3)

    n_sub = CMP_BLOCK // CMP_STRIDE
    assert n_sub == 2
    ncp = s // CMP_STRIDE
    chunks = jnp.stack([heads_first(kcm, g), heads_first(vcm, g)]).reshape(
        2, bsz, g, ncp, CMP_STRIDE * dk)
    w1 = w_cmp1.astype(BF16).reshape(2, n_sub, CMP_STRIDE * dk, dk)
    pe8 = jnp.broadcast_to(pe_cmp.astype(BF16).reshape(2, 1, CMP_BLOCK * dk),
                           (2, SUBLANES, CMP_BLOCK * dk))
    cmp_kv = _compress(chunks, w1, w_cmp2.astype(BF16), pe8)

    n_blk = s // SEL_BLOCK
    assert n_blk <= LANES and min(SEL_TOPK, n_blk) == SEL_TOPK
    cs = jnp.arange(ncp)[:, None] * CMP_STRIDE
    bs = jnp.arange(LANES)[None, :] * SEL_BLOCK
    ov = (jnp.clip(jnp.minimum(cs + CMP_BLOCK, bs + SEL_BLOCK) - jnp.maximum(cs, bs), 0, None)
          .astype(F32) / CMP_BLOCK).astype(BF16)

    o_nsa = _nsa(_alibi_slopes(NSA_HEADS), q_nsa, cmp_kv[0], cmp_kv[1], ks, vs, kw, vw,
                 gates, ov, tq, tk)
    o_nsa = o_nsa.transpose(0, 2, 1, 3).reshape(bsz, s, NSA_Q_W)

    def maps_first(a):
        return a.reshape(bsz, s, DIFF_HEADS, 2, dk).transpose(0, 2, 3, 1, 4)

    lam_init = 0.8 - 0.6 * math.exp(-0.3 * layer)
    o_diff = _diff(_alibi_slopes(DIFF_HEADS),
                   aug_q(maps_first(dq * scale), _alibi_slopes(DIFF_HEADS)),
                   aug_k(maps_first(dkk), False),
                   dv.reshape(bsz, s, DIFF_HEADS, DIFF_V_DIM).transpose(0, 2, 1, 3),
                   diff_lambda, diff_subln, min(DIFF_Q_TILE, tk), tk, lam_init)

    x1, hb, h_rows2d, logits = _outproj(o_nsa, o_diff, w_out.astype(BF16), x, g1, sc2, sh2,
                                        norm_g[1:2], norm_g[2:3], w_router, tm)

    oi, ow, cnt = _router(logits.reshape(n_tok, N_EXPERTS), router_bias, tm)
    idx_flat = oi[:, :TOP_K].reshape(-1)
    rank_flat = oi[:, TOP_K:2 * TOP_K].reshape(-1)
    w_flat = ow[:, :TOP_K].reshape(-1)
    counts = cnt[0].astype(jnp.int32)
    padded = (counts + EXPERT_CHUNK - 1) // EXPERT_CHUNK * EXPERT_CHUNK
    pend = jnp.cumsum(padded)
    pstart = (pend - padded).astype(jnp.int32)
    n_asg = n_tok * TOP_K
    n_chunks = (n_asg + N_EXPERTS * (EXPERT_CHUNK - 1) + EXPERT_CHUNK - 1) // EXPERT_CHUNK
    n_rows = n_chunks * EXPERT_CHUNK
    chunk_start = jnp.arange(n_chunks, dtype=jnp.int32) * EXPERT_CHUNK
    chunk_e = jnp.minimum(
        jnp.sum((pend[None, :] <= chunk_start[:, None]).astype(jnp.int32), axis=1), N_EXPERTS - 1)
    n_active = (pend[-1:] // EXPERT_CHUNK).astype(jnp.int32)
    first = jnp.concatenate([jnp.ones((1,), jnp.int32),
                             (chunk_e[1:] != chunk_e[:-1]).astype(jnp.int32)])
    slot = (jnp.cumsum(first) - 1) % 2
    e_ids = jnp.arange(N_EXPERTS, dtype=jnp.int32)
    from_here = lax.cummin(jnp.where(counts > 0, e_ids, N_EXPERTS), axis=0, reverse=True)
    after = jnp.concatenate([from_here[1:], jnp.full((1,), N_EXPERTS, jnp.int32)])
    next_e = after[chunk_e]
    next_e = jnp.where(next_e == N_EXPERTS, -1, next_e).astype(jnp.int32)

    row_tile = (d // LANES, LANES)
    xs = _dispatch(idx_flat, rank_flat, pstart, pend.astype(jnp.int32),
                   h_rows2d.reshape((n_tok,) + row_tile), n_rows, 128)
    ys = _experts(chunk_e, first, next_e, slot.astype(jnp.int32), n_active,
                  xs.reshape(n_rows * d // LANES, LANES), w_gate_up, w_down)
    routed = _combine(idx_flat, rank_flat, w_flat, pstart, ys.reshape((n_rows,) + row_tile),
                      n_tok, 128)
    return _final(hb, routed.reshape(n_tok * d // LANES, LANES), x1, g2, norm_g[3:4],
                  ws_gate_up.astype(BF16), ws_down.astype(BF16), tm)


def kernel(x, c, w_ada, b_ada, norm_g, w_in, w_cmp1, w_cmp2, pe_cmp, diff_lambda, diff_subln,
           w_out, w_router, router_bias, w_gate_up, w_down, ws_gate_up, ws_down):
    for layer in range(w_ada.shape[0]):
        x = _layer(x, c, w_ada[layer], b_ada[layer], norm_g[layer], w_in[layer], w_cmp1[layer],
                   w_cmp2[layer], pe_cmp[layer], diff_lambda[layer], diff_subln[layer],
                   w_out[layer], w_router[layer], router_bias[layer], w_gate_up[layer],
                   w_down[layer], ws_gate_up[layer], ws_down[layer], layer)
    return x
```

```python
import functools
import math

import jax
import jax.numpy as jnp
from jax import lax
from jax.experimental import pallas as pl
from jax.experimental.pallas import tpu as pltpu

HEAD_DIM = 64
NSA_HEADS = 8
NSA_KV_HEADS = 2
NSA_REP = NSA_HEADS // NSA_KV_HEADS
CMP_BLOCK = 32
CMP_STRIDE = 16
SEL_BLOCK = 64
SEL_TOPK = 16
WINDOW = 512
DIFF_HEADS = 4
DIFF_V_DIM = 2 * HEAD_DIM
N_EXPERTS = 256
TOP_K = 8
N_GROUPS = 8
TOPK_GROUPS = 4
EXPERT_DIM = 256
SHARED_DIM = 256
ROUTED_SCALE = 2.5
EXPERT_CHUNK = 128
RMS_EPS = 1e-6
NEG_INF = -1e30
BELOW_NEG_INF = -3e38
FORCE_BONUS = 1e4

NSA_Q_W = NSA_HEADS * HEAD_DIM
NSA_KV_W = NSA_KV_HEADS * HEAD_DIM
NSA_GATE_W = 3 * NSA_HEADS
DIFF_QK_W = DIFF_HEADS * 2 * HEAD_DIM
DIFF_V_W = DIFF_HEADS * DIFF_V_DIM
IN_SPLITS = (NSA_Q_W,) + (NSA_KV_W,) * 6 + (NSA_GATE_W, DIFF_QK_W, DIFF_QK_W, DIFF_V_W)
IN_WIDTH = sum(IN_SPLITS)

LANES = 128
SUBLANES = 8
VMEM_LIMIT = 56 * 1024 * 1024
KEY_TILE = 1024
DIFF_Q_TILE = 512
NSA_KEY_TILE = 1024

F32 = jnp.float32
BF16 = jnp.bfloat16


def _round_up(a, m):
    return (a + m - 1) // m * m


def _dot(a, b):
    return jnp.dot(a, b, preferred_element_type=F32)


def _dot_nt(a, b):
    return lax.dot_general(a, b, (((1,), (1,)), ((), ())), preferred_element_type=F32)


def _dot_split(a, b_bf16):
    hi = a.astype(BF16)
    lo = (a - hi.astype(F32)).astype(BF16)
    return _dot(hi, b_bf16) + _dot(lo, b_bf16)


def _dot_f32(a, b):
    a_hi = a.astype(BF16)
    a_lo = (a - a_hi.astype(F32)).astype(BF16)
    b_hi = b.astype(BF16)
    b_lo = (b - b_hi.astype(F32)).astype(BF16)
    return _dot(a_hi, b_hi) + (_dot(a_lo, b_hi) + _dot(a_hi, b_lo))


def _rms(x, g):
    return x * lax.rsqrt(jnp.mean(x * x, axis=-1, keepdims=True) + RMS_EPS) * g


def _params(*sem):
    return pltpu.CompilerParams(dimension_semantics=sem, vmem_limit_bytes=VMEM_LIMIT)


def _ada_kernel(c_ref, w_ref, b_ref, o_ref):
    c = c_ref[...]
    h = c * jax.nn.sigmoid(c)
    o_ref[...] = _dot_f32(h, w_ref[...]) + b_ref[...]


def _ada(c, w, b):
    bsz, d = c.shape
    n = w.shape[1]
    tn = n // 4
    cp = jnp.zeros((SUBLANES, d), F32).at[:bsz].set(c)
    out = pl.pallas_call(
        _ada_kernel,
        grid=(n // tn,),
        in_specs=[pl.BlockSpec((SUBLANES, d), lambda j: (0, 0)),
                  pl.BlockSpec((d, tn), lambda j: (0, j)),
                  pl.BlockSpec((1, tn), lambda j: (0, j))],
        out_specs=pl.BlockSpec((SUBLANES, tn), lambda j: (0, j)),
        out_shape=jax.ShapeDtypeStruct((SUBLANES, n), F32),
        compiler_params=_params("arbitrary"),
    )(cp, w, b.reshape(1, n))
    return out[:bsz]


def _inproj_kernel(x_ref, sc_ref, sh_ref, g_ref, w_ref, o_ref):
    h = _rms(x_ref[0], g_ref[...]) * (1.0 + sc_ref[0]) + sh_ref[0]
    o_ref[0] = _dot(h.astype(BF16), w_ref[...]).astype(o_ref.dtype)


def _inproj(x, sc, sh, g, w_bf16, tm):
    bsz, s, d = x.shape
    n = w_bf16.shape[1]
    return pl.pallas_call(
        _inproj_kernel,
        grid=(bsz, s // tm),
        in_specs=[pl.BlockSpec((1, tm, d), lambda b, i: (b, i, 0)),
                  pl.BlockSpec((1, 1, d), lambda b, i: (b, 0, 0)),
                  pl.BlockSpec((1, 1, d), lambda b, i: (b, 0, 0)),
                  pl.BlockSpec((1, d), lambda b, i: (0, 0)),
                  pl.BlockSpec((d, n), lambda b, i: (0, 0))],
        out_specs=pl.BlockSpec((1, tm, n), lambda b, i: (b, i, 0)),
        out_shape=jax.ShapeDtypeStruct((bsz, s, n), BF16),
        compiler_params=_params("parallel", "parallel"),
    )(x, sc, sh, g, w_bf16)


def _cmp_kernel(ch_ref, w1_ref, w2_ref, pe_ref, o_ref):
    ch = ch_ref[0, 0, 0]
    half = ch.shape[1]
    a = _dot(ch, w1_ref[0, 0])
    bm = _dot(ch, w1_ref[0, 1])
    n = bm.shape[0]
    bm_next = pltpu.roll(bm, shift=n - 1, axis=0)
    pe = pe_ref[0]
    pt = _dot(pe[:, :half], w1_ref[0, 0]) + _dot(pe[:, half:], w1_ref[0, 1])
    hmid = jax.nn.gelu(a + bm_next + pt[0:1])
    o_ref[0, 0, 0] = _dot(hmid.astype(BF16), w2_ref[0]).astype(o_ref.dtype)


def _compress(chunks, w1, w2, pe8):
    two, bsz, g, ncp, cw = chunks.shape
    dk = w2.shape[-1]
    return pl.pallas_call(
        _cmp_kernel,
        grid=(two, bsz, g),
        in_specs=[pl.BlockSpec((1, 1, 1, ncp, cw), lambda a, b, c: (a, b, c, 0, 0)),
                  pl.BlockSpec((1, 2, cw, dk), lambda a, b, c: (a, 0, 0, 0)),
                  pl.BlockSpec((1, dk, dk), lambda a, b, c: (a, 0, 0)),
                  pl.BlockSpec((1, SUBLANES, 2 * cw), lambda a, b, c: (a, 0, 0))],
        out_specs=pl.BlockSpec((1, 1, 1, ncp, dk), lambda a, b, c: (a, b, c, 0, 0)),
        out_shape=jax.ShapeDtypeStruct((two, bsz, g, ncp, dk), BF16),
        compiler_params=_params("parallel", "parallel", "parallel"),
    )(chunks, w1, w2, pe8)


AUG_W = LANES
POS_SPLIT = 64


def _flash_tile(s, delta, carry, v):
    m, l, acc = carry
    m_new = jnp.maximum(m, jnp.max(s, axis=-1, keepdims=True) + delta)
    pr = jnp.exp(s - (m_new - delta))
    alpha = jnp.exp(m - m_new)
    return (m_new, alpha * l + jnp.sum(pr, axis=-1, keepdims=True),
            alpha * acc + _dot(pr.astype(BF16), v))


def _nsa_kernel(slopes_ref, q_ref, kc_ref, vc_ref, ks_ref, vs_ref, kw_ref, vw_ref,
                g_ref, ov_ref, o_ref, bw_ref, m_ref, l_ref, acc_ref, *, tq, tk):
    g = pl.program_id(1)
    i = pl.program_id(2)
    rep = NSA_REP
    rows = rep * tq
    s0 = i * tq
    qa = q_ref[0].reshape(rows, AUG_W)
    q4 = qa[:, :HEAD_DIM]
    t = s0 + lax.broadcasted_iota(jnp.int32, (tq, 1), 0)
    t4 = jnp.concatenate([t] * rep, axis=0)
    slope4 = jnp.concatenate(
        [jnp.full((tq, 1), slopes_ref[g * rep + r], F32) for r in range(rep)], axis=0)
    wl = WINDOW + tq

    @pl.when(i <= WINDOW // tq)
    def _():
        row = lax.broadcasted_iota(jnp.int32, (tq, 1), 0)
        dw = (jnp.concatenate([row] * rep, axis=0)
              + (jnp.minimum(s0, WINDOW) - lax.broadcasted_iota(jnp.int32, (1, wl), 1)))
        bw_ref[...] = jnp.where((dw >= 0) & (dw < WINDOW), -slope4 * dw.astype(F32), NEG_INF)

    kc = kc_ref[0, 0]
    ncp = kc.shape[0]
    sc = _dot_nt(q4, kc)
    cend = lax.broadcasted_iota(jnp.int32, (1, ncp), 1) * CMP_STRIDE + (CMP_BLOCK - 1)
    dist = t4 - cend
    vis = dist >= 0
    sc = jnp.where(vis, sc - slope4 * dist.astype(F32), NEG_INF)
    e = jnp.exp(sc - jnp.max(sc, axis=-1, keepdims=True))
    p = e * jnp.where(t4 >= CMP_BLOCK - 1, 1.0 / jnp.sum(e, axis=-1, keepdims=True), 0.0)
    o_c = _dot(p.astype(BF16), vc_ref[0, 0])

    psum = p[0:tq]
    for r in range(1, rep):
        psum = psum + p[r * tq:(r + 1) * tq]
    imp = _dot_split(psum, ov_ref[...])
    jf = lax.broadcasted_iota(jnp.int32, (tq, LANES), 1)
    qb = t // SEL_BLOCK
    forced = (jf == 0) | (jf == qb) | (jf == qb - 1)
    v = jnp.where(jf <= qb, imp + jnp.where(forced, FORCE_BONUS, 0.0), NEG_INF)
    jff = jf.astype(F32)
    selb = jnp.full((tq, LANES), NEG_INF, F32)
    for _ in range(SEL_TOPK):
        vmax = jnp.max(v, axis=-1, keepdims=True)
        first = jnp.min(jnp.where(v == vmax, jff, float(LANES)), axis=-1, keepdims=True)
        pick = jff == first
        selb = jnp.where(pick, 0.0, selb)
        v = jnp.where(pick, BELOW_NEG_INF, v)
    chosen = jnp.max(jnp.where(selb == 0.0, 1.0, 0.0), axis=0, keepdims=True)
    selb = selb.astype(BF16)

    qs = jnp.concatenate([qa, jnp.concatenate([selb] * rep, axis=0)], axis=1)

    def sel_tile(kt, masked):
        k0 = pl.multiple_of(kt * tk, tk)
        s = _dot_nt(qs, ks_ref[0, 0, pl.ds(k0, tk), :])
        if masked:
            s = jnp.where(t4 >= k0 + lax.broadcasted_iota(jnp.int32, (1, tk), 1), s, NEG_INF)
        delta = slope4 * (k0 - s0).astype(F32)
        m_ref[...], l_ref[...], acc_ref[...] = _flash_tile(
            s, delta, (m_ref[...], l_ref[...], acc_ref[...]), vs_ref[0, 0, pl.ds(k0, tk), :])

    m_ref[...] = jnp.full((rows, 1), NEG_INF, F32)
    l_ref[...] = jnp.zeros((rows, 1), F32)
    acc_ref[...] = jnp.zeros((rows, HEAD_DIM), F32)
    blk_lane = lax.broadcasted_iota(jnp.int32, (1, LANES), 1)
    blk_per_tile = tk // SEL_BLOCK

    def sel_full_tile(kt, carry):
        in_tile = (blk_lane >= kt * blk_per_tile) & (blk_lane < (kt + 1) * blk_per_tile)

        @pl.when(jnp.max(jnp.where(in_tile, chosen, 0.0)) > 0.0)
        def _():
            sel_tile(kt, False)

        return carry

    n_full = s0 // tk
    lax.fori_loop(0, n_full, sel_full_tile, 0)
    sel_tile(n_full, True)
    o_s = acc_ref[...] / l_ref[...]

    ws = pl.multiple_of(jnp.maximum(s0 - WINDOW, 0), tq)
    sw = _dot_nt(q4, kw_ref[0, 0, pl.ds(ws, wl), :]) + bw_ref[...]
    ew = jnp.exp(sw - jnp.max(sw, axis=-1, keepdims=True))
    o_w = _dot(ew.astype(BF16), vw_ref[0, 0, pl.ds(ws, wl), :]) / jnp.sum(ew, axis=-1, keepdims=True)

    gate = jax.nn.sigmoid(g_ref[0, 0].astype(F32))
    for r in range(rep):
        rr = slice(r * tq, (r + 1) * tq)
        o = (gate[:, 3 * r:3 * r + 1] * o_c[rr] + gate[:, 3 * r + 1:3 * r + 2] * o_s[rr]
             + gate[:, 3 * r + 2:3 * r + 3] * o_w[rr])
        o_ref[0, r] = o.astype(o_ref.dtype)


def _nsa(slopes, q, kc, vc, ks, vs, kw, vw, gates, ov, tq, tk):
    bsz, h, s, _ = q.shape
    g = NSA_KV_HEADS
    rep = NSA_REP
    dk = HEAD_DIM
    assert s % tk == 0 and tk % tq == 0 and tq % SEL_BLOCK == 0 and tk // POS_SPLIT <= 256
    assert s >= WINDOW + tq and WINDOW % tq == 0

    def resident(a):
        return pl.BlockSpec((1, 1) + a.shape[2:], lambda b, c, i: (b, c, 0, 0))

    return pl.pallas_call(
        functools.partial(_nsa_kernel, tq=tq, tk=tk),
        grid=(bsz, g, s // tq),
        in_specs=[pl.BlockSpec(memory_space=pltpu.SMEM),
                  pl.BlockSpec((1, rep, tq, AUG_W), lambda b, c, i: (b, c, i, 0)),
                  resident(kc), resident(vc), resident(ks), resident(vs), resident(kw), resident(vw),
                  pl.BlockSpec((1, 1, tq, 3 * rep), lambda b, c, i: (b, c, i, 0)),
                  pl.BlockSpec(ov.shape, lambda b, c, i: (0, 0))],
        out_specs=pl.BlockSpec((1, rep, tq, dk), lambda b, c, i: (b, c, i, 0)),
        out_shape=jax.ShapeDtypeStruct((bsz, h, s, dk), BF16),
        scratch_shapes=[pltpu.VMEM((rep * tq, WINDOW + tq), F32),
                        pltpu.VMEM((rep * tq, 1), F32), pltpu.VMEM((rep * tq, 1), F32),
                        pltpu.VMEM((rep * tq, dk), F32)],
        compiler_params=_params("parallel", "parallel", "arbitrary"),
    )(slopes, q, kc, vc, ks, vs, kw, vw, gates, ov)


def _diff_kernel(slopes_ref, q_ref, k_ref, v_ref, lam_ref, g_ref, o_ref, *, tq, tk, lam_init):
    h = pl.program_id(1)
    i = pl.program_id(2)
    s0 = i * tq
    slope = slopes_ref[h]
    t = s0 + lax.broadcasted_iota(jnp.int32, (tq, 1), 0)

    def tile(kt, carry, masked):
        k0 = pl.multiple_of(kt * tk, tk)
        delta = slope * (k0 - s0).astype(F32)
        v = v_ref[0, 0, pl.ds(k0, tk), :]
        new = []
        for mi in range(2):
            s = _dot_nt(q_ref[0, 0, mi], k_ref[0, 0, mi, pl.ds(k0, tk), :])
            if masked:
                s = jnp.where(t >= k0 + lax.broadcasted_iota(jnp.int32, (1, tk), 1), s, NEG_INF)
            new.append(_flash_tile(s, delta, carry[mi], v))
        return tuple(new)

    one = (jnp.full((tq, 1), NEG_INF, F32), jnp.zeros((tq, 1), F32),
           jnp.zeros((tq, DIFF_V_DIM), F32))
    n_full = s0 // tk
    carry = lax.fori_loop(0, n_full, functools.partial(tile, masked=False), (one, one))
    (_, l0, acc0), (_, l1, acc1) = tile(n_full, carry, True)
    lp = lam_ref[...]
    lam = (jnp.exp(jnp.sum(lp[0:1] * lp[1:2], axis=-1, keepdims=True))
           - jnp.exp(jnp.sum(lp[2:3] * lp[3:4], axis=-1, keepdims=True)) + lam_init)
    o = acc0 / l0 - lam * (acc1 / l1)
    o_ref[0] = (_rms(o, g_ref[...]) * (1.0 - lam_init)).astype(o_ref.dtype)


def _diff(slopes, q, k, v, lam_params, subln_g, tq, tk, lam_init):
    bsz, h, _, s, _ = q.shape
    dv = v.shape[-1]
    assert s % tk == 0 and tk % tq == 0 and tk // POS_SPLIT <= 256
    return pl.pallas_call(
        functools.partial(_diff_kernel, tq=tq, tk=tk, lam_init=lam_init),
        grid=(bsz, h, s // tq),
        in_specs=[pl.BlockSpec(memory_space=pltpu.SMEM),
                  pl.BlockSpec((1, 1, 2, tq, AUG_W), lambda b, c, i: (b, c, 0, i, 0)),
                  pl.BlockSpec((1, 1, 2, s, AUG_W), lambda b, c, i: (b, c, 0, 0, 0)),
                  pl.BlockSpec((1, 1, s, dv), lambda b, c, i: (b, c, 0, 0)),
                  pl.BlockSpec(lam_params.shape, lambda b, c, i: (0, 0)),
                  pl.BlockSpec((1, dv), lambda b, c, i: (0, 0))],
        out_specs=pl.BlockSpec((1, tq, dv), lambda b, c, i: (b, i, c)),
        out_shape=jax.ShapeDtypeStruct((bsz, s, h * dv), BF16),
        compiler_params=_params("parallel", "parallel", "arbitrary"),
    )(slopes, q, k, v, lam_params, subln_g.reshape(1, dv))


def _outproj_kernel(a_ref, b_ref, wo_ref, x_ref, g1_ref, sc_ref, sh_ref, ng1_ref, ng2_ref,
                    wr_ref, x1_ref, hb_ref, hrow_ref, lg_ref):
    half = a_ref.shape[-1]
    tm = a_ref.shape[1]
    y = _dot(a_ref[0], wo_ref[0:half, :]) + _dot(b_ref[0], wo_ref[half:, :])
    x1 = x_ref[0] + g1_ref[0] * _rms(y, ng1_ref[...])
    x1_ref[0] = x1
    h = _rms(x1, ng2_ref[...]) * (1.0 + sc_ref[0]) + sh_ref[0]
    hb_ref[0] = h.astype(BF16)
    for s in range(h.shape[1] // LANES):
        hrow_ref[pl.ds(s, tm, stride=SUBLANES), :] = h[:, s * LANES:(s + 1) * LANES]
    lg_ref[0] = _dot_f32(h, wr_ref[...])


def _outproj(a, b, wo, x, g1, sc2, sh2, ng1, ng2, wr, tm):
    bsz, s, d = x.shape
    half = a.shape[-1]
    ne = wr.shape[1]
    nt = s // tm
    vec = pl.BlockSpec((1, 1, d), lambda bb, i: (bb, 0, 0))
    row = pl.BlockSpec((1, d), lambda bb, i: (0, 0))
    tile = pl.BlockSpec((1, tm, d), lambda bb, i: (bb, i, 0))
    return pl.pallas_call(
        _outproj_kernel,
        grid=(bsz, nt),
        in_specs=[pl.BlockSpec((1, tm, half), lambda bb, i: (bb, i, 0)),
                  pl.BlockSpec((1, tm, half), lambda bb, i: (bb, i, 0)),
                  pl.BlockSpec((d, d), lambda bb, i: (0, 0)),
                  tile, vec, vec, vec, row, row,
                  pl.BlockSpec((d, ne), lambda bb, i: (0, 0))],
        out_specs=[tile, tile,
                   pl.BlockSpec((tm * d // LANES, LANES), lambda bb, i: (bb * nt + i, 0)),
                   pl.BlockSpec((1, tm, ne), lambda bb, i: (bb, i, 0))],
        out_shape=[jax.ShapeDtypeStruct((bsz, s, d), F32),
                   jax.ShapeDtypeStruct((bsz, s, d), BF16),
                   jax.ShapeDtypeStruct((bsz * s * d // LANES, LANES), F32),
                   jax.ShapeDtypeStruct((bsz, s, ne), F32)],
        compiler_params=_params("parallel", "parallel"),
    )(a, b, wo, x, g1, sc2, sh2, ng1, ng2, wr)


def _router_kernel(lg_ref, bias_ref, tri_ref, oi_ref, ow_ref, cnt_ref, carry_ref):
    @pl.when(pl.program_id(0) == 0)
    def _():
        carry_ref[...] = jnp.zeros_like(carry_ref)

    tm, ne = lg_ref.shape
    per_grp = ne // N_GROUPS
    aff = jax.nn.sigmoid(lg_ref[...])
    choice = aff + bias_ref[...]
    lane = lax.broadcasted_iota(jnp.int32, (tm, ne), 1)
    lanef = lane.astype(F32)
    grp = lane // per_grp

    def row_max(a):
        return jnp.max(a, axis=-1, keepdims=True)

    def first_lane(eq):
        return jnp.min(jnp.where(eq, lanef, float(ne)), axis=-1, keepdims=True)

    gscore = []
    for gi in range(N_GROUPS):
        mg = jnp.where(grp == gi, choice, BELOW_NEG_INF)
        m1 = row_max(mg)
        mg = jnp.where(lanef == first_lane(mg == m1), BELOW_NEG_INF, mg)
        gscore.append(m1 + row_max(mg))
    keep = jnp.zeros((tm, ne), F32)
    for gi in range(N_GROUPS):
        beaten = jnp.zeros((tm, 1), F32)
        for gj in range(N_GROUPS):
            if gj == gi:
                continue
            wins = gscore[gj] >= gscore[gi] if gj < gi else gscore[gj] > gscore[gi]
            beaten = beaten + jnp.where(wins, 1.0, 0.0)
        keep = jnp.where(grp == gi, jnp.where(beaten < TOPK_GROUPS, 1.0, 0.0), keep)
    ch = jnp.where(keep > 0.5, choice, NEG_INF)

    idx, wts = [], []
    onehot = jnp.zeros((tm, ne), F32)
    for _ in range(TOP_K):
        first = first_lane(ch == row_max(ch))
        pick = lanef == first
        wts.append(jnp.sum(jnp.where(pick, aff, 0.0), axis=-1, keepdims=True))
        ch = jnp.where(pick, BELOW_NEG_INF, ch)
        onehot = jnp.where(pick, 1.0, onehot)
        idx.append(first)
    wsum = wts[0]
    for w in wts[1:]:
        wsum = wsum + w
    before = carry_ref[...] + _dot(tri_ref[...], onehot.astype(BF16))
    carry_ref[...] = carry_ref[...] + jnp.sum(onehot, axis=0, keepdims=True)
    cnt_ref[...] = jnp.broadcast_to(carry_ref[...], cnt_ref.shape)

    l128 = lax.broadcasted_iota(jnp.int32, (tm, LANES), 1)
    oi = jnp.zeros((tm, LANES), jnp.int32)
    ow = jnp.zeros((tm, LANES), F32)
    for k in range(TOP_K):
        rank = jnp.sum(jnp.where(lanef == idx[k], before, 0.0), axis=-1, keepdims=True)
        oi = jnp.where(l128 == k, idx[k].astype(jnp.int32), oi)
        oi = jnp.where(l128 == TOP_K + k, rank.astype(jnp.int32), oi)
        ow = jnp.where(l128 == k, wts[k] / wsum * ROUTED_SCALE, ow)
    oi_ref[...] = oi
    ow_ref[...] = ow


def _router(logits, bias, tm):
    n_tok, ne = logits.shape
    tri = (jnp.arange(tm)[:, None] > jnp.arange(tm)[None, :]).astype(BF16)
    return pl.pallas_call(
        _router_kernel,
        grid=(n_tok // tm,),
        in_specs=[pl.BlockSpec((tm, ne), lambda i: (i, 0)),
                  pl.BlockSpec((1, ne), lambda i: (0, 0)),
                  pl.BlockSpec((tm, tm), lambda i: (0, 0))],
        out_specs=[pl.BlockSpec((tm, LANES), lambda i: (i, 0)),
                   pl.BlockSpec((tm, LANES), lambda i: (i, 0)),
                   pl.BlockSpec((SUBLANES, ne), lambda i: (0, 0))],
        out_shape=[jax.ShapeDtypeStruct((n_tok, LANES), jnp.int32),
                   jax.ShapeDtypeStruct((n_tok, LANES), F32),
                   jax.ShapeDtypeStruct((SUBLANES, ne), F32)],
        scratch_shapes=[pltpu.VMEM((1, ne), F32)],
        compiler_params=_params("arbitrary"),
    )(logits, bias.reshape(1, ne), tri)


def _row_copy(src, src_row, dst, dst_row, sem):
    return pltpu.make_async_copy(src.at[src_row], dst.at[dst_row], sem)


def _chunk_fill(zbuf, xs_ref, start, sem):
    return pltpu.make_async_copy(zbuf, xs_ref.at[pl.ds(start, EXPERT_CHUNK)], sem)


DISPATCH_SLOTS = 3


def _dispatch_kernel(dest_ref, pstart_ref, pend_ref, h_hbm, xs_ref, zbuf, stage, sem, zsem,
                     lsem, *, th):
    step = pl.program_id(0)
    n_steps = pl.num_programs(0)

    @pl.when(step == 0)
    def _():
        zbuf[...] = jnp.zeros_like(zbuf)

        def fill(e, carry):
            @pl.when(pend_ref[e] > pstart_ref[e])
            def _():
                _chunk_fill(zbuf, xs_ref, pend_ref[e] - EXPERT_CHUNK, zsem).start()
            return carry

        lax.fori_loop(0, N_EXPERTS, fill, 0)

        def fill_done(e, carry):
            @pl.when(pend_ref[e] > pstart_ref[e])
            def _():
                _chunk_fill(zbuf, xs_ref, 0, zsem).wait()
            return carry

        lax.fori_loop(0, N_EXPERTS, fill_done, 0)

        first_unused = pend_ref[N_EXPERTS - 1] // EXPERT_CHUNK
        n_chunks = xs_ref.shape[0] // EXPERT_CHUNK

        def tail(ci, carry):
            _chunk_fill(zbuf, xs_ref, ci * EXPERT_CHUNK, zsem).start()
            return carry

        lax.fori_loop(first_unused, n_chunks, tail, 0)

        def tail_done(ci, carry):
            _chunk_fill(zbuf, xs_ref, 0, zsem).wait()
            return carry

        lax.fori_loop(first_unused, n_chunks, tail_done, 0)

    def load(s, slot):
        return pltpu.make_async_copy(h_hbm.at[pl.ds(s * th, th)], stage.at[slot], lsem.at[slot])

    @pl.when(step == 0)
    def _():
        load(0, 0).start()

    @pl.when(step + 1 < n_steps)
    def _():
        load(step + 1, (step + 1) % DISPATCH_SLOTS).start()

    slot = step % DISPATCH_SLOTS
    load(step, slot).wait()
    src = stage.at[slot]

    def issue(tl, carry):
        for k in range(TOP_K):
            a = tl * TOP_K + k
            _row_copy(src, tl, xs_ref, dest_ref[a], sem.at[step % 2]).start()
        return carry

    lax.fori_loop(0, th, issue, 0)

    def drain(parity):
        def body(tl, carry):
            for _ in range(TOP_K):
                _row_copy(src, 0, xs_ref, 0, sem.at[parity]).wait()
            return carry

        lax.fori_loop(0, th, body, 0)

    @pl.when(step > 0)
    def _():
        drain((step - 1) % 2)

    @pl.when(step == n_steps - 1)
    def _():
        drain(step % 2)


def _dispatch(dest_flat, pstart, pend, h_rows, n_rows, th):
    n_tok = h_rows.shape[0]
    row_shape = h_rows.shape[1:]
    smem_blk = pl.BlockSpec((th * TOP_K,), lambda i: (i,), memory_space=pltpu.SMEM)
    return pl.pallas_call(
        functools.partial(_dispatch_kernel, th=th),
        grid=(n_tok // th,),
        in_specs=[smem_blk,
                  pl.BlockSpec(memory_space=pltpu.SMEM),
                  pl.BlockSpec(memory_space=pltpu.SMEM),
                  pl.BlockSpec(memory_space=pl.ANY)],
        out_specs=pl.BlockSpec(memory_space=pl.ANY),
        out_shape=jax.ShapeDtypeStruct((n_rows,) + row_shape, h_rows.dtype),
        scratch_shapes=[pltpu.VMEM((EXPERT_CHUNK,) + row_shape, h_rows.dtype),
                        pltpu.VMEM((DISPATCH_SLOTS, th) + row_shape, h_rows.dtype),
                        pltpu.SemaphoreType.DMA((2,)), pltpu.SemaphoreType.DMA(()),
                        pltpu.SemaphoreType.DMA((DISPATCH_SLOTS,))],
        compiler_params=_params("arbitrary"),
    )(dest_flat, pstart, pend, h_rows)


def _weight_fetch(wgu_hbm, wdn_hbm, gu_buf, dn_buf, sems, expert, slot):
    return (pltpu.make_async_copy(wgu_hbm.at[expert], gu_buf.at[slot], sems.at[slot, 0]),
            pltpu.make_async_copy(wdn_hbm.at[expert], dn_buf.at[slot], sems.at[slot, 1]))


def _experts_kernel(ce_ref, first_ref, next_ref, slot_ref, na_ref, x_ref, wgu_hbm, wdn_hbm, o_ref,
                    gu_buf, dn_buf, gu_bf, dn_bf, sems):
    i = pl.program_id(0)
    ch = EXPERT_CHUNK
    nsl = x_ref.shape[0] // ch
    active = i < na_ref[0]
    fetch = functools.partial(_weight_fetch, wgu_hbm, wdn_hbm, gu_buf, dn_buf, sems)

    @pl.when(i == 0)
    def _():
        for cp in fetch(ce_ref[0], 0):
            cp.start()

    @pl.when(active & (first_ref[i] == 1))
    def _():
        slot = slot_ref[i]
        for cp in fetch(ce_ref[i], slot):
            cp.wait()

        @pl.when(next_ref[i] >= 0)
        def _():
            for cp in fetch(next_ref[i], 1 - slot):
                cp.start()

        gu_bf[...] = gu_buf[slot].astype(BF16)
        dn_bf[...] = dn_buf[slot].astype(BF16)

    @pl.when(active)
    def _():
        x = jnp.concatenate(
            [x_ref[pl.ds(s, ch, stride=SUBLANES), :] for s in range(nsl)], axis=1).astype(BF16)
        au = _dot(x, gu_bf[...])
        a = au[:, :EXPERT_DIM]
        u = au[:, EXPERT_DIM:]
        hmid = (a * jax.nn.sigmoid(a) * u).astype(BF16)
        y = _dot(hmid, dn_bf[...])
        for s in range(nsl):
            o_ref[pl.ds(s, ch, stride=SUBLANES), :] = y[:, s * LANES:(s + 1) * LANES]

    @pl.when(jnp.logical_not(active))
    def _():
        o_ref[...] = jnp.zeros_like(o_ref)


def _experts(chunk_e, first, next_e, slot, n_active, xs2d, w_gu, w_dn):
    n_chunks = chunk_e.shape[0]
    _, d, two_e = w_gu.shape
    blk_rows = EXPERT_CHUNK * d // LANES

    def x_map(i, ce, fs, nx, sl, na):
        return (jnp.minimum(i, na[0] - 1), 0)

    grid_spec = pltpu.PrefetchScalarGridSpec(
        num_scalar_prefetch=5,
        grid=(n_chunks,),
        in_specs=[pl.BlockSpec((blk_rows, LANES), x_map),
                  pl.BlockSpec(memory_space=pl.ANY),
                  pl.BlockSpec(memory_space=pl.ANY)],
        out_specs=pl.BlockSpec((blk_rows, LANES), lambda i, ce, fs, nx, sl, na: (i, 0)),
        scratch_shapes=[pltpu.VMEM((2, d, two_e), F32), pltpu.VMEM((2, two_e // 2, d), F32),
                        pltpu.VMEM((d, two_e), BF16), pltpu.VMEM((two_e // 2, d), BF16),
                        pltpu.SemaphoreType.DMA((2, 2))],
    )
    return pl.pallas_call(
        _experts_kernel,
        grid_spec=grid_spec,
        out_shape=jax.ShapeDtypeStruct(xs2d.shape, F32),
        compiler_params=_params("arbitrary"),
    )(chunk_e, first, next_e, slot, n_active, xs2d, w_gu, w_dn)


def _combine_kernel(dest_ref, dest_next_ref, w_ref, y_ref, o_ref, buf, sem, *, tj):
    step = pl.program_id(0)
    n_steps = pl.num_programs(0)
    cur = step % 2

    def gather(dest, slot):
        def issue(tl, carry):
            for k in range(TOP_K):
                a = tl * TOP_K + k
                _row_copy(y_ref, dest[a], buf.at[slot], a, sem.at[slot]).start()
            return carry

        lax.fori_loop(0, tj, issue, 0)

    @pl.when(step == 0)
    def _():
        gather(dest_ref, 0)

    @pl.when(step + 1 < n_steps)
    def _():
        gather(dest_next_ref, 1 - cur)

    rows = buf.at[cur]

    def drain(tl, carry):
        for _ in range(TOP_K):
            _row_copy(y_ref, 0, rows, 0, sem.at[cur]).wait()
        return carry

    lax.fori_loop(0, tj, drain, 0)

    def reduce(tl, carry):
        acc = w_ref[tl * TOP_K] * rows[tl * TOP_K]
        for k in range(1, TOP_K):
            acc = acc + w_ref[tl * TOP_K + k] * rows[tl * TOP_K + k]
        o_ref[tl] = acc
        return carry

    lax.fori_loop(0, tj, reduce, 0)


def _combine(dest_flat, w_flat, y_rows, n_tok, tj):
    row_shape = y_rows.shape[1:]
    n_steps = n_tok // tj
    smem_blk = pl.BlockSpec((tj * TOP_K,), lambda i: (i,), memory_space=pltpu.SMEM)
    smem_next = pl.BlockSpec((tj * TOP_K,), lambda i: (jnp.minimum(i + 1, n_steps - 1),),
                             memory_space=pltpu.SMEM)
    return pl.pallas_call(
        functools.partial(_combine_kernel, tj=tj),
        grid=(n_steps,),
        in_specs=[smem_blk, smem_next, smem_blk,
                  pl.BlockSpec(memory_space=pl.ANY)],
        out_specs=pl.BlockSpec((tj,) + row_shape, lambda i: (i, 0, 0)),
        out_shape=jax.ShapeDtypeStruct((n_tok,) + row_shape, F32),
        scratch_shapes=[pltpu.VMEM((2, tj * TOP_K) + row_shape, F32),
                        pltpu.SemaphoreType.DMA((2,))],
        compiler_params=_params("arbitrary"),
    )(dest_flat, dest_flat, w_flat, y_rows)


def _final_kernel(hb_ref, r_ref, x1_ref, g2_ref, ng_ref, wgu_ref, wdn_ref, o_ref):
    tm, d = hb_ref.shape[1:]
    au = _dot(hb_ref[0], wgu_ref[...])
    a = au[:, :SHARED_DIM]
    u = au[:, SHARED_DIM:]
    shared = _dot((a * jax.nn.sigmoid(a) * u).astype(BF16), wdn_ref[...])
    routed = jnp.concatenate(
        [r_ref[pl.ds(s, tm, stride=SUBLANES), :] for s in range(d // LANES)], axis=1)
    o_ref[0] = x1_ref[0] + g2_ref[0] * _rms(routed + shared, ng_ref[...])


def _final(hb, routed2d, x1, g2, ng, wgu, wdn, tm):
    bsz, s, d = x1.shape
    nt = s // tm
    tile = pl.BlockSpec((1, tm, d), lambda bb, i: (bb, i, 0))
    return pl.pallas_call(
        _final_kernel,
        grid=(bsz, nt),
        in_specs=[tile,
                  pl.BlockSpec((tm * d // LANES, LANES), lambda bb, i: (bb * nt + i, 0)),
                  tile,
                  pl.BlockSpec((1, 1, d), lambda bb, i: (bb, 0, 0)),
                  pl.BlockSpec((1, d), lambda bb, i: (0, 0)),
                  pl.BlockSpec(wgu.shape, lambda bb, i: (0, 0)),
                  pl.BlockSpec(wdn.shape, lambda bb, i: (0, 0))],
        out_specs=tile,
        out_shape=jax.ShapeDtypeStruct((bsz, s, d), F32),
        compiler_params=_params("parallel", "parallel"),
    )(hb, routed2d, x1, g2, ng, wgu, wdn)


def _alibi_slopes(n):
    return (2.0 ** (-8.0 * jnp.arange(1, n + 1, dtype=F32) / n)).astype(F32)


def _layer(x, c, w_ada, b_ada, norm_g, w_in, w_cmp1, w_cmp2, pe_cmp, diff_lambda, diff_subln,
           w_out, w_router, router_bias, w_gate_up, w_down, ws_gate_up, ws_down, layer):
    bsz, s, d = x.shape
    n_tok = bsz * s
    tm = 256
    tq = 256
    tk = min(KEY_TILE, s // 2)
    tk_nsa = min(NSA_KEY_TILE, s // 2)
    dk = HEAD_DIM
    g = NSA_KV_HEADS

    mod = _ada(c, w_ada, b_ada)
    sh1, sc1, g1, sh2, sc2, g2 = [m[:, None, :] for m in jnp.split(mod, 6, axis=-1)]

    n_pad = _round_up(IN_WIDTH, LANES)
    w_in_p = jnp.zeros((d, n_pad), BF16).at[:, :IN_WIDTH].set(w_in.astype(BF16))
    proj = _inproj(x, sc1, sh1, norm_g[0:1], w_in_p, tm)

    offs = [0]
    for wdt in IN_SPLITS:
        offs.append(offs[-1] + wdt)
    pieces = [proj[..., offs[j]:offs[j + 1]] for j in range(len(IN_SPLITS))]
    nq, kcm, vcm, ksel, vsel, kwin, vwin, ngate, dq, dkk, dv = pieces
    scale = dk ** -0.5

    def heads_first(a, nh):
        return a.reshape(bsz, s, nh, dk).transpose(0, 2, 1, 3)

    def aug_q(qh, slopes):
        shape = [1] * qh.ndim
        shape[1] = slopes.shape[0]
        lane = jnp.arange(AUG_W - dk)
        feat = jnp.where(lane == 0, POS_SPLIT, jnp.where(lane == 1, 1.0, 0.0)) * slopes[:, None]
        feat = feat.astype(BF16).reshape(shape[:-1] + [AUG_W - dk])
        return jnp.concatenate([qh, jnp.broadcast_to(feat, qh.shape[:-1] + (AUG_W - dk,))], axis=-1)

    def aug_k(kh, block_onehot, tile):
        col = (jnp.arange(s, dtype=jnp.int32) % tile)[:, None]
        lane = jnp.arange(AUG_W - dk)[None, :]
        feat = jnp.where(lane == 0, col // POS_SPLIT, jnp.where(lane == 1, col % POS_SPLIT, 0))
        parts = [kh, jnp.broadcast_to(feat.astype(BF16), kh.shape[:-1] + (AUG_W - dk,))]
        if block_onehot:
            oh = (jnp.arange(s)[:, None] // SEL_BLOCK == jnp.arange(LANES)[None, :]).astype(BF16)
            parts.append(jnp.broadcast_to(oh, kh.shape[:-1] + (LANES,)))
        return jnp.concatenate(parts, axis=-1)

    q_nsa = aug_q(heads_first(nq * scale, NSA_HEADS), _alibi_slopes(NSA_HEADS))
    ks = aug_k(heads_first(ksel, g), True, tk_nsa)
    vs, kw, vw = [heads_first(a, g) for a in (vsel, kwin, vwin)]
    gates = ngate.reshape(bsz, s, g, 3 * NSA_REP).transpose(0, 2, 1, 3)

    n_sub = CMP_BLOCK // CMP_STRIDE
    assert n_sub == 2
    ncp = s // CMP_STRIDE
    chunks = jnp.stack([heads_first(kcm, g), heads_first(vcm, g)]).reshape(
        2, bsz, g, ncp, CMP_STRIDE * dk)
    w1 = w_cmp1.astype(BF16).reshape(2, n_sub, CMP_STRIDE * dk, dk)
    pe8 = jnp.broadcast_to(pe_cmp.astype(BF16).reshape(2, 1, CMP_BLOCK * dk),
                           (2, SUBLANES, CMP_BLOCK * dk))
    cmp_kv = _compress(chunks, w1, w_cmp2.astype(BF16), pe8)

    n_blk = s // SEL_BLOCK
    assert n_blk <= LANES and min(SEL_TOPK, n_blk) == SEL_TOPK
    cs = jnp.arange(ncp)[:, None] * CMP_STRIDE
    bs = jnp.arange(LANES)[None, :] * SEL_BLOCK
    ov = (jnp.clip(jnp.minimum(cs + CMP_BLOCK, bs + SEL_BLOCK) - jnp.maximum(cs, bs), 0, None)
          .astype(F32) / CMP_BLOCK).astype(BF16)

    o_nsa = _nsa(_alibi_slopes(NSA_HEADS), q_nsa, cmp_kv[0], cmp_kv[1], ks, vs, kw, vw,
                 gates, ov, tq, tk_nsa)
    o_nsa = o_nsa.transpose(0, 2, 1, 3).reshape(bsz, s, NSA_Q_W)

    def maps_first(a):
        return a.reshape(bsz, s, DIFF_HEADS, 2, dk).transpose(0, 2, 3, 1, 4)

    lam_init = 0.8 - 0.6 * math.exp(-0.3 * layer)
    o_diff = _diff(_alibi_slopes(DIFF_HEADS),
                   aug_q(maps_first(dq * scale), _alibi_slopes(DIFF_HEADS)),
                   aug_k(maps_first(dkk), False, tk),
                   dv.reshape(bsz, s, DIFF_HEADS, DIFF_V_DIM).transpose(0, 2, 1, 3),
                   diff_lambda, diff_subln, min(DIFF_Q_TILE, tk), tk, lam_init)

    x1, hb, h_rows2d, logits = _outproj(o_nsa, o_diff, w_out.astype(BF16), x, g1, sc2, sh2,
                                        norm_g[1:2], norm_g[2:3], w_router, tm)

    oi, ow, cnt = _router(logits.reshape(n_tok, N_EXPERTS), router_bias, tm)
    idx_flat = oi[:, :TOP_K].reshape(-1)
    rank_flat = oi[:, TOP_K:2 * TOP_K].reshape(-1)
    w_flat = ow[:, :TOP_K].reshape(-1)
    counts = cnt[0].astype(jnp.int32)
    padded = (counts + EXPERT_CHUNK - 1) // EXPERT_CHUNK * EXPERT_CHUNK
    pend = jnp.cumsum(padded)
    pstart = (pend - padded).astype(jnp.int32)
    n_asg = n_tok * TOP_K
    n_chunks = (n_asg + N_EXPERTS * (EXPERT_CHUNK - 1) + EXPERT_CHUNK - 1) // EXPERT_CHUNK
    n_rows = n_chunks * EXPERT_CHUNK
    chunk_start = jnp.arange(n_chunks, dtype=jnp.int32) * EXPERT_CHUNK
    chunk_e = jnp.minimum(
        jnp.sum((pend[None, :] <= chunk_start[:, None]).astype(jnp.int32), axis=1), N_EXPERTS - 1)
    n_active = (pend[-1:] // EXPERT_CHUNK).astype(jnp.int32)
    first = jnp.concatenate([jnp.ones((1,), jnp.int32),
                             (chunk_e[1:] != chunk_e[:-1]).astype(jnp.int32)])
    slot = (jnp.cumsum(first) - 1) % 2
    e_ids = jnp.arange(N_EXPERTS, dtype=jnp.int32)
    from_here = lax.cummin(jnp.where(counts > 0, e_ids, N_EXPERTS), axis=0, reverse=True)
    after = jnp.concatenate([from_here[1:], jnp.full((1,), N_EXPERTS, jnp.int32)])
    next_e = after[chunk_e]
    next_e = jnp.where(next_e == N_EXPERTS, -1, next_e).astype(jnp.int32)

    row_tile = (d // LANES, LANES)
    dest_flat = pstart[idx_flat] + rank_flat
    xs = _dispatch(dest_flat, pstart, pend.astype(jnp.int32),
                   h_rows2d.reshape((n_tok,) + row_tile), n_rows, 128)
    ys = _experts(chunk_e, first, next_e, slot.astype(jnp.int32), n_active,
                  xs.reshape(n_rows * d // LANES, LANES), w_gate_up, w_down)
    routed = _combine(dest_flat, w_flat, ys.reshape((n_rows,) + row_tile), n_tok, 128)
    return _final(hb, routed.reshape(n_tok * d // LANES, LANES), x1, g2, norm_g[3:4],
                  ws_gate_up.astype(BF16), ws_down.astype(BF16), tm)


def kernel(x, c, w_ada, b_ada, norm_g, w_in, w_cmp1, w_cmp2, pe_cmp, diff_lambda, diff_subln,
           w_out, w_router, router_bias, w_gate_up, w_down, ws_gate_up, ws_down):
    for layer in range(w_ada.shape[0]):
        x = _layer(x, c, w_ada[layer], b_ada[layer], norm_g[layer], w_in[layer], w_cmp1[layer],
                   w_cmp2[layer], pe_cmp[layer], diff_lambda[layer], diff_subln[layer],
                   w_out[layer], w_router[layer], router_bias[layer], w_gate_up[layer],
                   w_down[layer], ws_gate_up[layer], ws_down[layer], layer)
    return x
```

```python
import functools
import math

import jax
import jax.numpy as jnp
from jax import lax
from jax.experimental import pallas as pl
from jax.experimental.pallas import tpu as pltpu

HEAD_DIM = 64
NSA_HEADS = 8
NSA_KV_HEADS = 2
NSA_REP = NSA_HEADS // NSA_KV_HEADS
CMP_BLOCK = 32
CMP_STRIDE = 16
SEL_BLOCK = 64
SEL_TOPK = 16
WINDOW = 512
DIFF_HEADS = 4
DIFF_V_DIM = 2 * HEAD_DIM
N_EXPERTS = 256
TOP_K = 8
N_GROUPS = 8
TOPK_GROUPS = 4
EXPERT_DIM = 256
SHARED_DIM = 256
ROUTED_SCALE = 2.5
EXPERT_CHUNK = 128
RMS_EPS = 1e-6
NEG_INF = -1e30
BELOW_NEG_INF = -3e38
FORCE_BONUS = 1e4

NSA_Q_W = NSA_HEADS * HEAD_DIM
NSA_KV_W = NSA_KV_HEADS * HEAD_DIM
NSA_GATE_W = 3 * NSA_HEADS
DIFF_QK_W = DIFF_HEADS * 2 * HEAD_DIM
DIFF_V_W = DIFF_HEADS * DIFF_V_DIM
IN_SPLITS = (NSA_Q_W,) + (NSA_KV_W,) * 6 + (NSA_GATE_W, DIFF_QK_W, DIFF_QK_W, DIFF_V_W)
IN_WIDTH = sum(IN_SPLITS)

LANES = 128
SUBLANES = 8
VMEM_LIMIT = 56 * 1024 * 1024
KEY_TILE = 1024
DIFF_Q_TILE = 512
NSA_KEY_TILE = 1024

F32 = jnp.float32
BF16 = jnp.bfloat16


def _round_up(a, m):
    return (a + m - 1) // m * m


def _dot(a, b):
    return jnp.dot(a, b, preferred_element_type=F32)


def _dot_nt(a, b):
    return lax.dot_general(a, b, (((1,), (1,)), ((), ())), preferred_element_type=F32)


def _dot_split(a, b_bf16):
    hi = a.astype(BF16)
    lo = (a - hi.astype(F32)).astype(BF16)
    return _dot(hi, b_bf16) + _dot(lo, b_bf16)


def _dot_f32(a, b):
    a_hi = a.astype(BF16)
    a_lo = (a - a_hi.astype(F32)).astype(BF16)
    b_hi = b.astype(BF16)
    b_lo = (b - b_hi.astype(F32)).astype(BF16)
    return _dot(a_hi, b_hi) + (_dot(a_lo, b_hi) + _dot(a_hi, b_lo))


def _rms(x, g):
    return x * lax.rsqrt(jnp.mean(x * x, axis=-1, keepdims=True) + RMS_EPS) * g


def _params(*sem):
    return pltpu.CompilerParams(dimension_semantics=sem, vmem_limit_bytes=VMEM_LIMIT)


def _ada_kernel(c_ref, w_ref, b_ref, o_ref):
    c = c_ref[...]
    h = c * jax.nn.sigmoid(c)
    o_ref[...] = _dot_f32(h, w_ref[...]) + b_ref[...]


def _ada(c, w, b):
    bsz, d = c.shape
    n = w.shape[1]
    tn = n // 4
    cp = jnp.zeros((SUBLANES, d), F32).at[:bsz].set(c)
    out = pl.pallas_call(
        _ada_kernel,
        grid=(n // tn,),
        in_specs=[pl.BlockSpec((SUBLANES, d), lambda j: (0, 0)),
                  pl.BlockSpec((d, tn), lambda j: (0, j)),
                  pl.BlockSpec((1, tn), lambda j: (0, j))],
        out_specs=pl.BlockSpec((SUBLANES, tn), lambda j: (0, j)),
        out_shape=jax.ShapeDtypeStruct((SUBLANES, n), F32),
        compiler_params=_params("arbitrary"),
    )(cp, w, b.reshape(1, n))
    return out[:bsz]


def _inproj_kernel(x_ref, sc_ref, sh_ref, g_ref, w_ref, o_ref):
    h = _rms(x_ref[0], g_ref[...]) * (1.0 + sc_ref[0]) + sh_ref[0]
    o_ref[0] = _dot(h.astype(BF16), w_ref[...]).astype(o_ref.dtype)


def _inproj(x, sc, sh, g, w_bf16, tm):
    bsz, s, d = x.shape
    n = w_bf16.shape[1]
    return pl.pallas_call(
        _inproj_kernel,
        grid=(bsz, s // tm),
        in_specs=[pl.BlockSpec((1, tm, d), lambda b, i: (b, i, 0)),
                  pl.BlockSpec((1, 1, d), lambda b, i: (b, 0, 0)),
                  pl.BlockSpec((1, 1, d), lambda b, i: (b, 0, 0)),
                  pl.BlockSpec((1, d), lambda b, i: (0, 0)),
                  pl.BlockSpec((d, n), lambda b, i: (0, 0))],
        out_specs=pl.BlockSpec((1, tm, n), lambda b, i: (b, i, 0)),
        out_shape=jax.ShapeDtypeStruct((bsz, s, n), BF16),
        compiler_params=_params("parallel", "parallel"),
    )(x, sc, sh, g, w_bf16)


def _cmp_kernel(ch_ref, w1_ref, w2_ref, pe_ref, o_ref):
    ch = ch_ref[0, 0, 0]
    half = ch.shape[1]
    a = _dot(ch, w1_ref[0, 0])
    bm = _dot(ch, w1_ref[0, 1])
    n = bm.shape[0]
    bm_next = pltpu.roll(bm, shift=n - 1, axis=0)
    pe = pe_ref[0]
    pt = _dot(pe[:, :half], w1_ref[0, 0]) + _dot(pe[:, half:], w1_ref[0, 1])
    hmid = jax.nn.gelu(a + bm_next + pt[0:1])
    o_ref[0, 0, 0] = _dot(hmid.astype(BF16), w2_ref[0]).astype(o_ref.dtype)


def _compress(chunks, w1, w2, pe8):
    two, bsz, g, ncp, cw = chunks.shape
    dk = w2.shape[-1]
    return pl.pallas_call(
        _cmp_kernel,
        grid=(two, bsz, g),
        in_specs=[pl.BlockSpec((1, 1, 1, ncp, cw), lambda a, b, c: (a, b, c, 0, 0)),
                  pl.BlockSpec((1, 2, cw, dk), lambda a, b, c: (a, 0, 0, 0)),
                  pl.BlockSpec((1, dk, dk), lambda a, b, c: (a, 0, 0)),
                  pl.BlockSpec((1, SUBLANES, 2 * cw), lambda a, b, c: (a, 0, 0))],
        out_specs=pl.BlockSpec((1, 1, 1, ncp, dk), lambda a, b, c: (a, b, c, 0, 0)),
        out_shape=jax.ShapeDtypeStruct((two, bsz, g, ncp, dk), BF16),
        compiler_params=_params("parallel", "parallel", "parallel"),
    )(chunks, w1, w2, pe8)


AUG_W = LANES
POS_SPLIT = 64


def _flash_tile(s, delta, carry, v):
    m, l, acc = carry
    m_new = jnp.maximum(m, jnp.max(s, axis=-1, keepdims=True) + delta)
    pr = jnp.exp(s - (m_new - delta))
    alpha = jnp.exp(m - m_new)
    return (m_new, alpha * l + jnp.sum(pr, axis=-1, keepdims=True),
            alpha * acc + _dot(pr.astype(BF16), v))


def _nsa_kernel(slopes_ref, q_ref, kc_ref, vc_ref, ks_ref, vs_ref, kw_ref, vw_ref,
                g_ref, ov_ref, o_ref, bw_ref, m_ref, l_ref, acc_ref, *, tq, tk):
    g = pl.program_id(1)
    i = pl.program_id(2)
    rep = NSA_REP
    rows = rep * tq
    s0 = i * tq
    qa = q_ref[0].reshape(rows, AUG_W)
    q4 = qa[:, :HEAD_DIM]
    t = s0 + lax.broadcasted_iota(jnp.int32, (tq, 1), 0)
    t4 = jnp.concatenate([t] * rep, axis=0)
    slope4 = jnp.concatenate(
        [jnp.full((tq, 1), slopes_ref[g * rep + r], F32) for r in range(rep)], axis=0)
    wl = WINDOW + tq

    @pl.when(i <= WINDOW // tq)
    def _():
        row = lax.broadcasted_iota(jnp.int32, (tq, 1), 0)
        dw = (jnp.concatenate([row] * rep, axis=0)
              + (jnp.minimum(s0, WINDOW) - lax.broadcasted_iota(jnp.int32, (1, wl), 1)))
        bw_ref[...] = jnp.where((dw >= 0) & (dw < WINDOW), -slope4 * dw.astype(F32), NEG_INF)

    kc = kc_ref[0, 0]
    ncp = kc.shape[0]
    sc = _dot_nt(q4, kc)
    cend = lax.broadcasted_iota(jnp.int32, (1, ncp), 1) * CMP_STRIDE + (CMP_BLOCK - 1)
    dist = t4 - cend
    vis = dist >= 0
    sc = jnp.where(vis, sc - slope4 * dist.astype(F32), NEG_INF)
    e = jnp.exp(sc - jnp.max(sc, axis=-1, keepdims=True))
    p = e * jnp.where(t4 >= CMP_BLOCK - 1, 1.0 / jnp.sum(e, axis=-1, keepdims=True), 0.0)
    o_c = _dot(p.astype(BF16), vc_ref[0, 0])

    psum = p[0:tq]
    for r in range(1, rep):
        psum = psum + p[r * tq:(r + 1) * tq]
    imp = _dot_split(psum, ov_ref[...])
    jf = lax.broadcasted_iota(jnp.int32, (tq, LANES), 1)
    qb = t // SEL_BLOCK
    forced = (jf == 0) | (jf == qb) | (jf == qb - 1)
    v = jnp.where(jf <= qb, imp + jnp.where(forced, FORCE_BONUS, 0.0), NEG_INF)
    jff = jf.astype(F32)
    selb = jnp.full((tq, LANES), NEG_INF, F32)
    for _ in range(SEL_TOPK):
        vmax = jnp.max(v, axis=-1, keepdims=True)
        first = jnp.min(jnp.where(v == vmax, jff, float(LANES)), axis=-1, keepdims=True)
        pick = jff == first
        selb = jnp.where(pick, 0.0, selb)
        v = jnp.where(pick, BELOW_NEG_INF, v)
    chosen = jnp.max(jnp.where(selb == 0.0, 1.0, 0.0), axis=0, keepdims=True)
    selb = selb.astype(BF16)

    qs = jnp.concatenate([qa, jnp.concatenate([selb] * rep, axis=0)], axis=1)

    def sel_tile(kt, masked):
        k0 = pl.multiple_of(kt * tk, tk)
        s = _dot_nt(qs, ks_ref[0, 0, pl.ds(k0, tk), :])
        if masked:
            s = jnp.where(t4 >= k0 + lax.broadcasted_iota(jnp.int32, (1, tk), 1), s, NEG_INF)
        delta = slope4 * (k0 - s0).astype(F32)
        m_ref[...], l_ref[...], acc_ref[...] = _flash_tile(
            s, delta, (m_ref[...], l_ref[...], acc_ref[...]), vs_ref[0, 0, pl.ds(k0, tk), :])

    m_ref[...] = jnp.full((rows, 1), NEG_INF, F32)
    l_ref[...] = jnp.zeros((rows, 1), F32)
    acc_ref[...] = jnp.zeros((rows, HEAD_DIM), F32)
    blk_lane = lax.broadcasted_iota(jnp.int32, (1, LANES), 1)
    blk_per_tile = tk // SEL_BLOCK

    def sel_full_tile(kt, carry):
        in_tile = (blk_lane >= kt * blk_per_tile) & (blk_lane < (kt + 1) * blk_per_tile)

        @pl.when(jnp.max(jnp.where(in_tile, chosen, 0.0)) > 0.0)
        def _():
            sel_tile(kt, False)

        return carry

    n_full = s0 // tk
    lax.fori_loop(0, n_full, sel_full_tile, 0)
    sel_tile(n_full, True)
    o_s = acc_ref[...] / l_ref[...]

    ws = pl.multiple_of(jnp.maximum(s0 - WINDOW, 0), tq)
    sw = _dot_nt(q4, kw_ref[0, 0, pl.ds(ws, wl), :]) + bw_ref[...]
    ew = jnp.exp(sw - jnp.max(sw, axis=-1, keepdims=True))
    o_w = _dot(ew.astype(BF16), vw_ref[0, 0, pl.ds(ws, wl), :]) / jnp.sum(ew, axis=-1, keepdims=True)

    gate = jax.nn.sigmoid(g_ref[0, 0].astype(F32))
    for r in range(rep):
        rr = slice(r * tq, (r + 1) * tq)
        o = (gate[:, 3 * r:3 * r + 1] * o_c[rr] + gate[:, 3 * r + 1:3 * r + 2] * o_s[rr]
             + gate[:, 3 * r + 2:3 * r + 3] * o_w[rr])
        o_ref[0, r] = o.astype(o_ref.dtype)


def _nsa(slopes, q, kc, vc, ks, vs, kw, vw, gates, ov, tq, tk):
    bsz, h, s, _ = q.shape
    g = NSA_KV_HEADS
    rep = NSA_REP
    dk = HEAD_DIM
    assert s % tk == 0 and tk % tq == 0 and tq % SEL_BLOCK == 0 and tk // POS_SPLIT <= 256
    assert s >= WINDOW + tq and WINDOW % tq == 0

    def resident(a):
        return pl.BlockSpec((1, 1) + a.shape[2:], lambda b, c, i: (b, c, 0, 0))

    return pl.pallas_call(
        functools.partial(_nsa_kernel, tq=tq, tk=tk),
        grid=(bsz, g, s // tq),
        in_specs=[pl.BlockSpec(memory_space=pltpu.SMEM),
                  pl.BlockSpec((1, rep, tq, AUG_W), lambda b, c, i: (b, c, i, 0)),
                  resident(kc), resident(vc), resident(ks), resident(vs), resident(kw), resident(vw),
                  pl.BlockSpec((1, 1, tq, 3 * rep), lambda b, c, i: (b, c, i, 0)),
                  pl.BlockSpec(ov.shape, lambda b, c, i: (0, 0))],
        out_specs=pl.BlockSpec((1, rep, tq, dk), lambda b, c, i: (b, c, i, 0)),
        out_shape=jax.ShapeDtypeStruct((bsz, h, s, dk), BF16),
        scratch_shapes=[pltpu.VMEM((rep * tq, WINDOW + tq), F32),
                        pltpu.VMEM((rep * tq, 1), F32), pltpu.VMEM((rep * tq, 1), F32),
                        pltpu.VMEM((rep * tq, dk), F32)],
        compiler_params=_params("parallel", "parallel", "arbitrary"),
    )(slopes, q, kc, vc, ks, vs, kw, vw, gates, ov)


def _diff_kernel(slopes_ref, q_ref, k_ref, v_ref, lam_ref, g_ref, o_ref, *, tq, tk, lam_init):
    h = pl.program_id(1)
    i = pl.program_id(2)
    s0 = i * tq
    slope = slopes_ref[h]
    t = s0 + lax.broadcasted_iota(jnp.int32, (tq, 1), 0)

    def tile(kt, carry, masked):
        k0 = pl.multiple_of(kt * tk, tk)
        delta = slope * (k0 - s0).astype(F32)
        v = v_ref[0, 0, pl.ds(k0, tk), :]
        new = []
        for mi in range(2):
            s = _dot_nt(q_ref[0, 0, mi], k_ref[0, 0, mi, pl.ds(k0, tk), :])
            if masked:
                s = jnp.where(t >= k0 + lax.broadcasted_iota(jnp.int32, (1, tk), 1), s, NEG_INF)
            new.append(_flash_tile(s, delta, carry[mi], v))
        return tuple(new)

    one = (jnp.full((tq, 1), NEG_INF, F32), jnp.zeros((tq, 1), F32),
           jnp.zeros((tq, DIFF_V_DIM), F32))
    n_full = s0 // tk
    carry = lax.fori_loop(0, n_full, functools.partial(tile, masked=False), (one, one))
    (_, l0, acc0), (_, l1, acc1) = tile(n_full, carry, True)
    lp = lam_ref[...]
    lam = (jnp.exp(jnp.sum(lp[0:1] * lp[1:2], axis=-1, keepdims=True))
           - jnp.exp(jnp.sum(lp[2:3] * lp[3:4], axis=-1, keepdims=True)) + lam_init)
    o = acc0 / l0 - lam * (acc1 / l1)
    o_ref[0] = (_rms(o, g_ref[...]) * (1.0 - lam_init)).astype(o_ref.dtype)


def _diff(slopes, q, k, v, lam_params, subln_g, tq, tk, lam_init):
    bsz, h, _, s, _ = q.shape
    dv = v.shape[-1]
    assert s % tk == 0 and tk % tq == 0 and tk // POS_SPLIT <= 256
    return pl.pallas_call(
        functools.partial(_diff_kernel, tq=tq, tk=tk, lam_init=lam_init),
        grid=(bsz, h, s // tq),
        in_specs=[pl.BlockSpec(memory_space=pltpu.SMEM),
                  pl.BlockSpec((1, 1, 2, tq, AUG_W), lambda b, c, i: (b, c, 0, i, 0)),
                  pl.BlockSpec((1, 1, 2, s, AUG_W), lambda b, c, i: (b, c, 0, 0, 0)),
                  pl.BlockSpec((1, 1, s, dv), lambda b, c, i: (b, c, 0, 0)),
                  pl.BlockSpec(lam_params.shape, lambda b, c, i: (0, 0)),
                  pl.BlockSpec((1, dv), lambda b, c, i: (0, 0))],
        out_specs=pl.BlockSpec((1, tq, dv), lambda b, c, i: (b, i, c)),
        out_shape=jax.ShapeDtypeStruct((bsz, s, h * dv), BF16),
        compiler_params=_params("parallel", "parallel", "arbitrary"),
    )(slopes, q, k, v, lam_params, subln_g.reshape(1, dv))


def _outproj_kernel(a_ref, b_ref, wo_ref, x_ref, g1_ref, sc_ref, sh_ref, ng1_ref, ng2_ref,
                    wr_ref, x1_ref, hb_ref, hrow_ref, lg_ref):
    half = a_ref.shape[-1]
    tm = a_ref.shape[1]
    y = _dot(a_ref[0], wo_ref[0:half, :]) + _dot(b_ref[0], wo_ref[half:, :])
    x1 = x_ref[0] + g1_ref[0] * _rms(y, ng1_ref[...])
    x1_ref[0] = x1
    h = _rms(x1, ng2_ref[...]) * (1.0 + sc_ref[0]) + sh_ref[0]
    hb_ref[0] = h.astype(BF16)
    for s in range(h.shape[1] // LANES):
        hrow_ref[pl.ds(s, tm, stride=SUBLANES), :] = h[:, s * LANES:(s + 1) * LANES]
    lg_ref[0] = _dot_f32(h, wr_ref[...])


def _outproj(a, b, wo, x, g1, sc2, sh2, ng1, ng2, wr, tm):
    bsz, s, d = x.shape
    half = a.shape[-1]
    ne = wr.shape[1]
    nt = s // tm
    vec = pl.BlockSpec((1, 1, d), lambda bb, i: (bb, 0, 0))
    row = pl.BlockSpec((1, d), lambda bb, i: (0, 0))
    tile = pl.BlockSpec((1, tm, d), lambda bb, i: (bb, i, 0))
    return pl.pallas_call(
        _outproj_kernel,
        grid=(bsz, nt),
        in_specs=[pl.BlockSpec((1, tm, half), lambda bb, i: (bb, i, 0)),
                  pl.BlockSpec((1, tm, half), lambda bb, i: (bb, i, 0)),
                  pl.BlockSpec((d, d), lambda bb, i: (0, 0)),
                  tile, vec, vec, vec, row, row,
                  pl.BlockSpec((d, ne), lambda bb, i: (0, 0))],
        out_specs=[tile, tile,
                   pl.BlockSpec((tm * d // LANES, LANES), lambda bb, i: (bb * nt + i, 0)),
                   pl.BlockSpec((1, tm, ne), lambda bb, i: (bb, i, 0))],
        out_shape=[jax.ShapeDtypeStruct((bsz, s, d), F32),
                   jax.ShapeDtypeStruct((bsz, s, d), BF16),
                   jax.ShapeDtypeStruct((bsz * s * d // LANES, LANES), F32),
                   jax.ShapeDtypeStruct((bsz, s, ne), F32)],
        compiler_params=_params("parallel", "parallel"),
    )(a, b, wo, x, g1, sc2, sh2, ng1, ng2, wr)


def _router_kernel(lg_ref, bias_ref, tri_ref, oi_ref, ow_ref, cnt_ref, carry_ref):
    @pl.when(pl.program_id(0) == 0)
    def _():
        carry_ref[...] = jnp.zeros_like(carry_ref)

    tm, ne = lg_ref.shape
    per_grp = ne // N_GROUPS
    aff = jax.nn.sigmoid(lg_ref[...])
    choice = aff + bias_ref[...]
    lane = lax.broadcasted_iota(jnp.int32, (tm, ne), 1)
    lanef = lane.astype(F32)
    grp = lane // per_grp

    def row_max(a):
        return jnp.max(a, axis=-1, keepdims=True)

    def first_lane(eq):
        return jnp.min(jnp.where(eq, lanef, float(ne)), axis=-1, keepdims=True)

    gscore = []
    for gi in range(N_GROUPS):
        mg = jnp.where(grp == gi, choice, BELOW_NEG_INF)
        m1 = row_max(mg)
        mg = jnp.where(lanef == first_lane(mg == m1), BELOW_NEG_INF, mg)
        gscore.append(m1 + row_max(mg))
    keep = jnp.zeros((tm, ne), F32)
    for gi in range(N_GROUPS):
        beaten = jnp.zeros((tm, 1), F32)
        for gj in range(N_GROUPS):
            if gj == gi:
                continue
            wins = gscore[gj] >= gscore[gi] if gj < gi else gscore[gj] > gscore[gi]
            beaten = beaten + jnp.where(wins, 1.0, 0.0)
        keep = jnp.where(grp == gi, jnp.where(beaten < TOPK_GROUPS, 1.0, 0.0), keep)
    ch = jnp.where(keep > 0.5, choice, NEG_INF)

    idx, wts = [], []
    onehot = jnp.zeros((tm, ne), F32)
    for _ in range(TOP_K):
        first = first_lane(ch == row_max(ch))
        pick = lanef == first
        wts.append(jnp.sum(jnp.where(pick, aff, 0.0), axis=-1, keepdims=True))
        ch = jnp.where(pick, BELOW_NEG_INF, ch)
        onehot = jnp.where(pick, 1.0, onehot)
        idx.append(first)
    wsum = wts[0]
    for w in wts[1:]:
        wsum = wsum + w
    before = carry_ref[...] + _dot(tri_ref[...], onehot.astype(BF16))
    carry_ref[...] = carry_ref[...] + jnp.sum(onehot, axis=0, keepdims=True)
    cnt_ref[...] = jnp.broadcast_to(carry_ref[...], cnt_ref.shape)

    l128 = lax.broadcasted_iota(jnp.int32, (tm, LANES), 1)
    oi = jnp.zeros((tm, LANES), jnp.int32)
    ow = jnp.zeros((tm, LANES), F32)
    for k in range(TOP_K):
        rank = jnp.sum(jnp.where(lanef == idx[k], before, 0.0), axis=-1, keepdims=True)
        oi = jnp.where(l128 == k, idx[k].astype(jnp.int32), oi)
        oi = jnp.where(l128 == TOP_K + k, rank.astype(jnp.int32), oi)
        ow = jnp.where(l128 == k, wts[k] / wsum * ROUTED_SCALE, ow)
    oi_ref[...] = oi
    ow_ref[...] = ow


def _router(logits, bias, tm):
    n_tok, ne = logits.shape
    tri = (jnp.arange(tm)[:, None] > jnp.arange(tm)[None, :]).astype(BF16)
    return pl.pallas_call(
        _router_kernel,
        grid=(n_tok // tm,),
        in_specs=[pl.BlockSpec((tm, ne), lambda i: (i, 0)),
                  pl.BlockSpec((1, ne), lambda i: (0, 0)),
                  pl.BlockSpec((tm, tm), lambda i: (0, 0))],
        out_specs=[pl.BlockSpec((tm, LANES), lambda i: (i, 0)),
                   pl.BlockSpec((tm, LANES), lambda i: (i, 0)),
                   pl.BlockSpec((SUBLANES, ne), lambda i: (0, 0))],
        out_shape=[jax.ShapeDtypeStruct((n_tok, LANES), jnp.int32),
                   jax.ShapeDtypeStruct((n_tok, LANES), F32),
                   jax.ShapeDtypeStruct((SUBLANES, ne), F32)],
        scratch_shapes=[pltpu.VMEM((1, ne), F32)],
        compiler_params=_params("arbitrary"),
    )(logits, bias.reshape(1, ne), tri)


def _row_copy(src, src_row, dst, dst_row, sem):
    return pltpu.make_async_copy(src.at[src_row], dst.at[dst_row], sem)


def _chunk_fill(zbuf, xs_ref, start, sem):
    return pltpu.make_async_copy(zbuf, xs_ref.at[pl.ds(start, EXPERT_CHUNK)], sem)


DISPATCH_SLOTS = 3
CHUNKS_PER_STEP = 4


def _dispatch_kernel(dest_ref, pstart_ref, pend_ref, h_hbm, xs_ref, zbuf, stage, sem, zsem,
                     lsem, *, th):
    step = pl.program_id(0)
    n_steps = pl.num_programs(0)

    @pl.when(step == 0)
    def _():
        zbuf[...] = jnp.zeros_like(zbuf)

        def fill(e, carry):
            @pl.when(pend_ref[e] > pstart_ref[e])
            def _():
                _chunk_fill(zbuf, xs_ref, pend_ref[e] - EXPERT_CHUNK, zsem).start()
            return carry

        lax.fori_loop(0, N_EXPERTS, fill, 0)

        def fill_done(e, carry):
            @pl.when(pend_ref[e] > pstart_ref[e])
            def _():
                _chunk_fill(zbuf, xs_ref, 0, zsem).wait()
            return carry

        lax.fori_loop(0, N_EXPERTS, fill_done, 0)

        first_unused = pend_ref[N_EXPERTS - 1] // EXPERT_CHUNK
        n_chunks = xs_ref.shape[0] // EXPERT_CHUNK

        def tail(ci, carry):
            _chunk_fill(zbuf, xs_ref, ci * EXPERT_CHUNK, zsem).start()
            return carry

        lax.fori_loop(first_unused, n_chunks, tail, 0)

        def tail_done(ci, carry):
            _chunk_fill(zbuf, xs_ref, 0, zsem).wait()
            return carry

        lax.fori_loop(first_unused, n_chunks, tail_done, 0)

    def load(s, slot):
        return pltpu.make_async_copy(h_hbm.at[pl.ds(s * th, th)], stage.at[slot], lsem.at[slot])

    @pl.when(step == 0)
    def _():
        load(0, 0).start()

    @pl.when(step + 1 < n_steps)
    def _():
        load(step + 1, (step + 1) % DISPATCH_SLOTS).start()

    slot = step % DISPATCH_SLOTS
    load(step, slot).wait()
    src = stage.at[slot]

    def issue(tl, carry):
        for k in range(TOP_K):
            a = tl * TOP_K + k
            _row_copy(src, tl, xs_ref, dest_ref[a], sem.at[step % 2]).start()
        return carry

    lax.fori_loop(0, th, issue, 0)

    def drain(parity):
        def body(tl, carry):
            for _ in range(TOP_K):
                _row_copy(src, 0, xs_ref, 0, sem.at[parity]).wait()
            return carry

        lax.fori_loop(0, th, body, 0)

    @pl.when(step > 0)
    def _():
        drain((step - 1) % 2)

    @pl.when(step == n_steps - 1)
    def _():
        drain(step % 2)


def _dispatch(dest_flat, pstart, pend, h_rows, n_rows, th):
    n_tok = h_rows.shape[0]
    row_shape = h_rows.shape[1:]
    smem_blk = pl.BlockSpec((th * TOP_K,), lambda i: (i,), memory_space=pltpu.SMEM)
    return pl.pallas_call(
        functools.partial(_dispatch_kernel, th=th),
        grid=(n_tok // th,),
        in_specs=[smem_blk,
                  pl.BlockSpec(memory_space=pltpu.SMEM),
                  pl.BlockSpec(memory_space=pltpu.SMEM),
                  pl.BlockSpec(memory_space=pl.ANY)],
        out_specs=pl.BlockSpec(memory_space=pl.ANY),
        out_shape=jax.ShapeDtypeStruct((n_rows,) + row_shape, h_rows.dtype),
        scratch_shapes=[pltpu.VMEM((EXPERT_CHUNK,) + row_shape, h_rows.dtype),
                        pltpu.VMEM((DISPATCH_SLOTS, th) + row_shape, h_rows.dtype),
                        pltpu.SemaphoreType.DMA((2,)), pltpu.SemaphoreType.DMA(()),
                        pltpu.SemaphoreType.DMA((DISPATCH_SLOTS,))],
        compiler_params=_params("arbitrary"),
    )(dest_flat, pstart, pend, h_rows)


def _weight_fetch(wgu_hbm, wdn_hbm, gu_buf, dn_buf, sems, expert, slot):
    return (pltpu.make_async_copy(wgu_hbm.at[expert], gu_buf.at[slot], sems.at[slot, 0]),
            pltpu.make_async_copy(wdn_hbm.at[expert], dn_buf.at[slot], sems.at[slot, 1]))


def _experts_kernel(ce_ref, first_ref, next_ref, slot_ref, na_ref, x_ref, wgu_hbm, wdn_hbm, o_ref,
                    gu_buf, dn_buf, gu_bf, dn_bf, sems):
    ch = EXPERT_CHUNK
    nsl = wgu_hbm.shape[1] // LANES
    chunk_rows = ch * nsl
    fetch = functools.partial(_weight_fetch, wgu_hbm, wdn_hbm, gu_buf, dn_buf, sems)

    @pl.when(pl.program_id(0) == 0)
    def _():
        for cp in fetch(ce_ref[0], 0):
            cp.start()

    for sub in range(CHUNKS_PER_STEP):
        i = pl.program_id(0) * CHUNKS_PER_STEP + sub
        active = i < na_ref[0]
        base = sub * chunk_rows

        @pl.when(active & (first_ref[i] == 1))
        def _():
            slot = slot_ref[i]
            for cp in fetch(ce_ref[i], slot):
                cp.wait()

            @pl.when(next_ref[i] >= 0)
            def _():
                for cp in fetch(next_ref[i], 1 - slot):
                    cp.start()

            gu_bf[...] = gu_buf[slot].astype(BF16)
            dn_bf[...] = dn_buf[slot].astype(BF16)

        @pl.when(active)
        def _():
            x = jnp.concatenate(
                [x_ref[pl.ds(base + s, ch, stride=SUBLANES), :] for s in range(nsl)],
                axis=1).astype(BF16)
            au = _dot(x, gu_bf[...])
            a = au[:, :EXPERT_DIM]
            u = au[:, EXPERT_DIM:]
            hmid = (a * jax.nn.sigmoid(a) * u).astype(BF16)
            y = _dot(hmid, dn_bf[...])
            for s in range(nsl):
                o_ref[pl.ds(base + s, ch, stride=SUBLANES), :] = y[:, s * LANES:(s + 1) * LANES]

        @pl.when(jnp.logical_not(active))
        def _():
            o_ref[pl.ds(base, chunk_rows), :] = jnp.zeros((chunk_rows, LANES), F32)


def _experts(chunk_e, first, next_e, slot, n_active, xs2d, w_gu, w_dn):
    n_chunks = chunk_e.shape[0]
    _, d, two_e = w_gu.shape
    blk_rows = CHUNKS_PER_STEP * EXPERT_CHUNK * d // LANES
    assert n_chunks % CHUNKS_PER_STEP == 0

    def x_map(i, ce, fs, nx, sl, na):
        return (jnp.minimum(i, (na[0] - 1) // CHUNKS_PER_STEP), 0)

    grid_spec = pltpu.PrefetchScalarGridSpec(
        num_scalar_prefetch=5,
        grid=(n_chunks // CHUNKS_PER_STEP,),
        in_specs=[pl.BlockSpec((blk_rows, LANES), x_map),
                  pl.BlockSpec(memory_space=pl.ANY),
                  pl.BlockSpec(memory_space=pl.ANY)],
        out_specs=pl.BlockSpec((blk_rows, LANES), lambda i, ce, fs, nx, sl, na: (i, 0)),
        scratch_shapes=[pltpu.VMEM((2, d, two_e), F32), pltpu.VMEM((2, two_e // 2, d), F32),
                        pltpu.VMEM((d, two_e), BF16), pltpu.VMEM((two_e // 2, d), BF16),
                        pltpu.SemaphoreType.DMA((2, 2))],
    )
    return pl.pallas_call(
        _experts_kernel,
        grid_spec=grid_spec,
        out_shape=jax.ShapeDtypeStruct(xs2d.shape, F32),
        compiler_params=_params("arbitrary"),
    )(chunk_e, first, next_e, slot, n_active, xs2d, w_gu, w_dn)


def _combine_kernel(dest_ref, dest_next_ref, w_ref, y_ref, o_ref, buf, sem, *, tj):
    step = pl.program_id(0)
    n_steps = pl.num_programs(0)
    cur = step % 2

    def gather(dest, slot):
        def issue(tl, carry):
            for k in range(TOP_K):
                a = tl * TOP_K + k
                _row_copy(y_ref, dest[a], buf.at[slot], a, sem.at[slot]).start()
            return carry

        lax.fori_loop(0, tj, issue, 0)

    @pl.when(step == 0)
    def _():
        gather(dest_ref, 0)

    @pl.when(step + 1 < n_steps)
    def _():
        gather(dest_next_ref, 1 - cur)

    rows = buf.at[cur]

    def drain(tl, carry):
        for _ in range(TOP_K):
            _row_copy(y_ref, 0, rows, 0, sem.at[cur]).wait()
        return carry

    lax.fori_loop(0, tj, drain, 0)

    def reduce(tl, carry):
        acc = w_ref[tl * TOP_K] * rows[tl * TOP_K]
        for k in range(1, TOP_K):
            acc = acc + w_ref[tl * TOP_K + k] * rows[tl * TOP_K + k]
        o_ref[tl] = acc
        return carry

    lax.fori_loop(0, tj, reduce, 0)


def _combine(dest_flat, w_flat, y_rows, n_tok, tj):
    row_shape = y_rows.shape[1:]
    n_steps = n_tok // tj
    smem_blk = pl.BlockSpec((tj * TOP_K,), lambda i: (i,), memory_space=pltpu.SMEM)
    smem_next = pl.BlockSpec((tj * TOP_K,), lambda i: (jnp.minimum(i + 1, n_steps - 1),),
                             memory_space=pltpu.SMEM)
    return pl.pallas_call(
        functools.partial(_combine_kernel, tj=tj),
        grid=(n_steps,),
        in_specs=[smem_blk, smem_next, smem_blk,
                  pl.BlockSpec(memory_space=pl.ANY)],
        out_specs=pl.BlockSpec((tj,) + row_shape, lambda i: (i, 0, 0)),
        out_shape=jax.ShapeDtypeStruct((n_tok,) + row_shape, F32),
        scratch_shapes=[pltpu.VMEM((2, tj * TOP_K) + row_shape, F32),
                        pltpu.SemaphoreType.DMA((2,))],
        compiler_params=_params("arbitrary"),
    )(dest_flat, dest_flat, w_flat, y_rows)


def _final_kernel(hb_ref, r_ref, x1_ref, g2_ref, ng_ref, wgu_ref, wdn_ref, o_ref):
    tm, d = hb_ref.shape[1:]
    au = _dot(hb_ref[0], wgu_ref[...])
    a = au[:, :SHARED_DIM]
    u = au[:, SHARED_DIM:]
    shared = _dot((a * jax.nn.sigmoid(a) * u).astype(BF16), wdn_ref[...])
    routed = jnp.concatenate(
        [r_ref[pl.ds(s, tm, stride=SUBLANES), :] for s in range(d // LANES)], axis=1)
    o_ref[0] = x1_ref[0] + g2_ref[0] * _rms(routed + shared, ng_ref[...])


def _final(hb, routed2d, x1, g2, ng, wgu, wdn, tm):
    bsz, s, d = x1.shape
    nt = s // tm
    tile = pl.BlockSpec((1, tm, d), lambda bb, i: (bb, i, 0))
    return pl.pallas_call(
        _final_kernel,
        grid=(bsz, nt),
        in_specs=[tile,
                  pl.BlockSpec((tm * d // LANES, LANES), lambda bb, i: (bb * nt + i, 0)),
                  tile,
                  pl.BlockSpec((1, 1, d), lambda bb, i: (bb, 0, 0)),
                  pl.BlockSpec((1, d), lambda bb, i: (0, 0)),
                  pl.BlockSpec(wgu.shape, lambda bb, i: (0, 0)),
                  pl.BlockSpec(wdn.shape, lambda bb, i: (0, 0))],
        out_specs=tile,
        out_shape=jax.ShapeDtypeStruct((bsz, s, d), F32),
        compiler_params=_params("parallel", "parallel"),
    )(hb, routed2d, x1, g2, ng, wgu, wdn)


def _alibi_slopes(n):
    return (2.0 ** (-8.0 * jnp.arange(1, n + 1, dtype=F32) / n)).astype(F32)


def _layer(x, c, w_ada, b_ada, norm_g, w_in, w_cmp1, w_cmp2, pe_cmp, diff_lambda, diff_subln,
           w_out, w_router, router_bias, w_gate_up, w_down, ws_gate_up, ws_down, layer):
    bsz, s, d = x.shape
    n_tok = bsz * s
    tm = 256
    tq = 256
    tk = min(KEY_TILE, s // 2)
    tk_nsa = min(NSA_KEY_TILE, s // 2)
    dk = HEAD_DIM
    g = NSA_KV_HEADS

    mod = _ada(c, w_ada, b_ada)
    sh1, sc1, g1, sh2, sc2, g2 = [m[:, None, :] for m in jnp.split(mod, 6, axis=-1)]

    n_pad = _round_up(IN_WIDTH, LANES)
    w_in_p = jnp.zeros((d, n_pad), BF16).at[:, :IN_WIDTH].set(w_in.astype(BF16))
    proj = _inproj(x, sc1, sh1, norm_g[0:1], w_in_p, tm)

    offs = [0]
    for wdt in IN_SPLITS:
        offs.append(offs[-1] + wdt)
    pieces = [proj[..., offs[j]:offs[j + 1]] for j in range(len(IN_SPLITS))]
    nq, kcm, vcm, ksel, vsel, kwin, vwin, ngate, dq, dkk, dv = pieces
    scale = dk ** -0.5

    def heads_first(a, nh):
        return a.reshape(bsz, s, nh, dk).transpose(0, 2, 1, 3)

    def aug_q(qh, slopes):
        shape = [1] * qh.ndim
        shape[1] = slopes.shape[0]
        lane = jnp.arange(AUG_W - dk)
        feat = jnp.where(lane == 0, POS_SPLIT, jnp.where(lane == 1, 1.0, 0.0)) * slopes[:, None]
        feat = feat.astype(BF16).reshape(shape[:-1] + [AUG_W - dk])
        return jnp.concatenate([qh, jnp.broadcast_to(feat, qh.shape[:-1] + (AUG_W - dk,))], axis=-1)

    def aug_k(kh, block_onehot, tile):
        col = (jnp.arange(s, dtype=jnp.int32) % tile)[:, None]
        lane = jnp.arange(AUG_W - dk)[None, :]
        feat = jnp.where(lane == 0, col // POS_SPLIT, jnp.where(lane == 1, col % POS_SPLIT, 0))
        parts = [kh, jnp.broadcast_to(feat.astype(BF16), kh.shape[:-1] + (AUG_W - dk,))]
        if block_onehot:
            oh = (jnp.arange(s)[:, None] // SEL_BLOCK == jnp.arange(LANES)[None, :]).astype(BF16)
            parts.append(jnp.broadcast_to(oh, kh.shape[:-1] + (LANES,)))
        return jnp.concatenate(parts, axis=-1)

    q_nsa = aug_q(heads_first(nq * scale, NSA_HEADS), _alibi_slopes(NSA_HEADS))
    ks = aug_k(heads_first(ksel, g), True, tk_nsa)
    vs, kw, vw = [heads_first(a, g) for a in (vsel, kwin, vwin)]
    gates = ngate.reshape(bsz, s, g, 3 * NSA_REP).transpose(0, 2, 1, 3)

    n_sub = CMP_BLOCK // CMP_STRIDE
    assert n_sub == 2
    ncp = s // CMP_STRIDE
    chunks = jnp.stack([heads_first(kcm, g), heads_first(vcm, g)]).reshape(
        2, bsz, g, ncp, CMP_STRIDE * dk)
    w1 = w_cmp1.astype(BF16).reshape(2, n_sub, CMP_STRIDE * dk, dk)
    pe8 = jnp.broadcast_to(pe_cmp.astype(BF16).reshape(2, 1, CMP_BLOCK * dk),
                           (2, SUBLANES, CMP_BLOCK * dk))
    cmp_kv = _compress(chunks, w1, w_cmp2.astype(BF16), pe8)

    n_blk = s // SEL_BLOCK
    assert n_blk <= LANES and min(SEL_TOPK, n_blk) == SEL_TOPK
    cs = jnp.arange(ncp)[:, None] * CMP_STRIDE
    bs = jnp.arange(LANES)[None, :] * SEL_BLOCK
    ov = (jnp.clip(jnp.minimum(cs + CMP_BLOCK, bs + SEL_BLOCK) - jnp.maximum(cs, bs), 0, None)
          .astype(F32) / CMP_BLOCK).astype(BF16)

    o_nsa = _nsa(_alibi_slopes(NSA_HEADS), q_nsa, cmp_kv[0], cmp_kv[1], ks, vs, kw, vw,
                 gates, ov, tq, tk_nsa)
    o_nsa = o_nsa.transpose(0, 2, 1, 3).reshape(bsz, s, NSA_Q_W)

    def maps_first(a):
        return a.reshape(bsz, s, DIFF_HEADS, 2, dk).transpose(0, 2, 3, 1, 4)

    lam_init = 0.8 - 0.6 * math.exp(-0.3 * layer)
    o_diff = _diff(_alibi_slopes(DIFF_HEADS),
                   aug_q(maps_first(dq * scale), _alibi_slopes(DIFF_HEADS)),
                   aug_k(maps_first(dkk), False, tk),
                   dv.reshape(bsz, s, DIFF_HEADS, DIFF_V_DIM).transpose(0, 2, 1, 3),
                   diff_lambda, diff_subln, min(DIFF_Q_TILE, tk), tk, lam_init)

    x1, hb, h_rows2d, logits = _outproj(o_nsa, o_diff, w_out.astype(BF16), x, g1, sc2, sh2,
                                        norm_g[1:2], norm_g[2:3], w_router, tm)

    oi, ow, cnt = _router(logits.reshape(n_tok, N_EXPERTS), router_bias, tm)
    idx_flat = oi[:, :TOP_K].reshape(-1)
    rank_flat = oi[:, TOP_K:2 * TOP_K].reshape(-1)
    w_flat = ow[:, :TOP_K].reshape(-1)
    counts = cnt[0].astype(jnp.int32)
    padded = (counts + EXPERT_CHUNK - 1) // EXPERT_CHUNK * EXPERT_CHUNK
    pend = jnp.cumsum(padded)
    pstart = (pend - padded).astype(jnp.int32)
    n_asg = n_tok * TOP_K
    n_chunks = (n_asg + N_EXPERTS * (EXPERT_CHUNK - 1) + EXPERT_CHUNK - 1) // EXPERT_CHUNK
    n_chunks = _round_up(n_chunks, CHUNKS_PER_STEP)
    n_rows = n_chunks * EXPERT_CHUNK
    chunk_start = jnp.arange(n_chunks, dtype=jnp.int32) * EXPERT_CHUNK
    chunk_e = jnp.minimum(
        jnp.sum((pend[None, :] <= chunk_start[:, None]).astype(jnp.int32), axis=1), N_EXPERTS - 1)
    n_active = (pend[-1:] // EXPERT_CHUNK).astype(jnp.int32)
    first = jnp.concatenate([jnp.ones((1,), jnp.int32),
                             (chunk_e[1:] != chunk_e[:-1]).astype(jnp.int32)])
    slot = (jnp.cumsum(first) - 1) % 2
    e_ids = jnp.arange(N_EXPERTS, dtype=jnp.int32)
    from_here = lax.cummin(jnp.where(counts > 0, e_ids, N_EXPERTS), axis=0, reverse=True)
    after = jnp.concatenate([from_here[1:], jnp.full((1,), N_EXPERTS, jnp.int32)])
    next_e = after[chunk_e]
    next_e = jnp.where(next_e == N_EXPERTS, -1, next_e).astype(jnp.int32)

    row_tile = (d // LANES, LANES)
    dest_flat = rank_flat + jnp.sum(
        jnp.where(idx_flat[:, None] == e_ids[None, :], pstart[None, :], 0), axis=1)
    xs = _dispatch(dest_flat, pstart, pend.astype(jnp.int32),
                   h_rows2d.reshape((n_tok,) + row_tile), n_rows, 128)
    ys = _experts(chunk_e, first, next_e, slot.astype(jnp.int32), n_active,
                  xs.reshape(n_rows * d // LANES, LANES), w_gate_up, w_down)
    routed = _combine(dest_flat, w_flat, ys.reshape((n_rows,) + row_tile), n_tok, 128)
    return _final(hb, routed.reshape(n_tok * d // LANES, LANES), x1, g2, norm_g[3:4],
                  ws_gate_up.astype(BF16), ws_down.astype(BF16), tm)


def kernel(x, c, w_ada, b_ada, norm_g, w_in, w_cmp1, w_cmp2, pe_cmp, diff_lambda, diff_subln,
           w_out, w_router, router_bias, w_gate_up, w_down, ws_gate_up, ws_down):
    for layer in range(w_ada.shape[0]):
        x = _layer(x, c, w_ada[layer], b_ada[layer], norm_g[layer], w_in[layer], w_cmp1[layer],
                   w_cmp2[layer], pe_cmp[layer], diff_lambda[layer], diff_subln[layer],
                   w_out[layer], w_router[layer], router_bias[layer], w_gate_up[layer],
                   w_down[layer], ws_gate_up[layer], ws_down[layer], layer)
    return x
```

```python
import functools
import math

import jax
import jax.numpy as jnp
from jax import lax
from jax.experimental import pallas as pl
from jax.experimental.pallas import tpu as pltpu

HEAD_DIM = 64
NSA_HEADS = 8
NSA_KV_HEADS = 2
NSA_REP = NSA_HEADS // NSA_KV_HEADS
CMP_BLOCK = 32
CMP_STRIDE = 16
SEL_BLOCK = 64
SEL_TOPK = 16
WINDOW = 512
DIFF_HEADS = 4
DIFF_V_DIM = 2 * HEAD_DIM
N_EXPERTS = 256
TOP_K = 8
N_GROUPS = 8
TOPK_GROUPS = 4
EXPERT_DIM = 256
SHARED_DIM = 256
ROUTED_SCALE = 2.5
EXPERT_CHUNK = 128
RMS_EPS = 1e-6
NEG_INF = -1e30
BELOW_NEG_INF = -3e38
FORCE_BONUS = 1e4

NSA_Q_W = NSA_HEADS * HEAD_DIM
NSA_KV_W = NSA_KV_HEADS * HEAD_DIM
NSA_GATE_W = 3 * NSA_HEADS
DIFF_QK_W = DIFF_HEADS * 2 * HEAD_DIM
DIFF_V_W = DIFF_HEADS * DIFF_V_DIM
IN_SPLITS = (NSA_Q_W,) + (NSA_KV_W,) * 6 + (NSA_GATE_W, DIFF_QK_W, DIFF_QK_W, DIFF_V_W)
IN_WIDTH = sum(IN_SPLITS)

LANES = 128
SUBLANES = 8
VMEM_LIMIT = 56 * 1024 * 1024
KEY_TILE = 1024
DIFF_Q_TILE = 512
NSA_KEY_TILE = 1024

F32 = jnp.float32
BF16 = jnp.bfloat16


def _round_up(a, m):
    return (a + m - 1) // m * m


def _dot(a, b):
    return jnp.dot(a, b, preferred_element_type=F32)


def _dot_nt(a, b):
    return lax.dot_general(a, b, (((1,), (1,)), ((), ())), preferred_element_type=F32)


def _dot_split(a, b_bf16):
    hi = a.astype(BF16)
    lo = (a - hi.astype(F32)).astype(BF16)
    return _dot(hi, b_bf16) + _dot(lo, b_bf16)


def _dot_f32(a, b):
    a_hi = a.astype(BF16)
    a_lo = (a - a_hi.astype(F32)).astype(BF16)
    b_hi = b.astype(BF16)
    b_lo = (b - b_hi.astype(F32)).astype(BF16)
    return _dot(a_hi, b_hi) + (_dot(a_lo, b_hi) + _dot(a_hi, b_lo))


def _rms(x, g):
    return x * lax.rsqrt(jnp.mean(x * x, axis=-1, keepdims=True) + RMS_EPS) * g


def _params(*sem):
    return pltpu.CompilerParams(dimension_semantics=sem, vmem_limit_bytes=VMEM_LIMIT)


def _ada_kernel(c_ref, w_ref, b_ref, o_ref):
    c = c_ref[...]
    h = c * jax.nn.sigmoid(c)
    o_ref[...] = _dot_f32(h, w_ref[...]) + b_ref[...]


def _ada(c, w, b):
    bsz, d = c.shape
    n = w.shape[1]
    tn = n // 4
    cp = jnp.zeros((SUBLANES, d), F32).at[:bsz].set(c)
    out = pl.pallas_call(
        _ada_kernel,
        grid=(n // tn,),
        in_specs=[pl.BlockSpec((SUBLANES, d), lambda j: (0, 0)),
                  pl.BlockSpec((d, tn), lambda j: (0, j)),
                  pl.BlockSpec((1, tn), lambda j: (0, j))],
        out_specs=pl.BlockSpec((SUBLANES, tn), lambda j: (0, j)),
        out_shape=jax.ShapeDtypeStruct((SUBLANES, n), F32),
        compiler_params=_params("arbitrary"),
    )(cp, w, b.reshape(1, n))
    return out[:bsz]


def _inproj_kernel(x_ref, sc_ref, sh_ref, g_ref, w_ref, o_ref):
    h = _rms(x_ref[0], g_ref[...]) * (1.0 + sc_ref[0]) + sh_ref[0]
    o_ref[0] = _dot(h.astype(BF16), w_ref[...]).astype(o_ref.dtype)


def _inproj(x, sc, sh, g, w_bf16, tm):
    bsz, s, d = x.shape
    n = w_bf16.shape[1]
    return pl.pallas_call(
        _inproj_kernel,
        grid=(bsz, s // tm),
        in_specs=[pl.BlockSpec((1, tm, d), lambda b, i: (b, i, 0)),
                  pl.BlockSpec((1, 1, d), lambda b, i: (b, 0, 0)),
                  pl.BlockSpec((1, 1, d), lambda b, i: (b, 0, 0)),
                  pl.BlockSpec((1, d), lambda b, i: (0, 0)),
                  pl.BlockSpec((d, n), lambda b, i: (0, 0))],
        out_specs=pl.BlockSpec((1, tm, n), lambda b, i: (b, i, 0)),
        out_shape=jax.ShapeDtypeStruct((bsz, s, n), BF16),
        compiler_params=_params("parallel", "parallel"),
    )(x, sc, sh, g, w_bf16)


def _cmp_kernel(ch_ref, w1_ref, w2_ref, pe_ref, o_ref):
    ch = ch_ref[0, 0, 0]
    half = ch.shape[1]
    a = _dot(ch, w1_ref[0, 0])
    bm = _dot(ch, w1_ref[0, 1])
    n = bm.shape[0]
    bm_next = pltpu.roll(bm, shift=n - 1, axis=0)
    pe = pe_ref[0]
    pt = _dot(pe[:, :half], w1_ref[0, 0]) + _dot(pe[:, half:], w1_ref[0, 1])
    hmid = jax.nn.gelu(a + bm_next + pt[0:1])
    o_ref[0, 0, 0] = _dot(hmid.astype(BF16), w2_ref[0]).astype(o_ref.dtype)


def _compress(chunks, w1, w2, pe8):
    two, bsz, g, ncp, cw = chunks.shape
    dk = w2.shape[-1]
    return pl.pallas_call(
        _cmp_kernel,
        grid=(two, bsz, g),
        in_specs=[pl.BlockSpec((1, 1, 1, ncp, cw), lambda a, b, c: (a, b, c, 0, 0)),
                  pl.BlockSpec((1, 2, cw, dk), lambda a, b, c: (a, 0, 0, 0)),
                  pl.BlockSpec((1, dk, dk), lambda a, b, c: (a, 0, 0)),
                  pl.BlockSpec((1, SUBLANES, 2 * cw), lambda a, b, c: (a, 0, 0))],
        out_specs=pl.BlockSpec((1, 1, 1, ncp, dk), lambda a, b, c: (a, b, c, 0, 0)),
        out_shape=jax.ShapeDtypeStruct((two, bsz, g, ncp, dk), BF16),
        compiler_params=_params("parallel", "parallel", "parallel"),
    )(chunks, w1, w2, pe8)


AUG_W = LANES
POS_SPLIT = 64


def _flash_tile(s, delta, carry, v):
    m, l, acc = carry
    m_new = jnp.maximum(m, jnp.max(s, axis=-1, keepdims=True) + delta)
    pr = jnp.exp(s - (m_new - delta))
    alpha = jnp.exp(m - m_new)
    return (m_new, alpha * l + jnp.sum(pr, axis=-1, keepdims=True),
            alpha * acc + _dot(pr.astype(BF16), v))


def _nsa_kernel(slopes_ref, q_ref, kc_ref, vc_ref, ks_ref, vs_ref, kw_ref, vw_ref,
                g_ref, ov_ref, o_ref, bw_ref, m_ref, l_ref, acc_ref, *, tq, tk):
    g = pl.program_id(1)
    i = pl.program_id(2)
    rep = NSA_REP
    rows = rep * tq
    s0 = i * tq
    qa = q_ref[0].reshape(rows, AUG_W)
    q4 = qa[:, :HEAD_DIM]
    t = s0 + lax.broadcasted_iota(jnp.int32, (tq, 1), 0)
    t4 = jnp.concatenate([t] * rep, axis=0)
    slope4 = jnp.concatenate(
        [jnp.full((tq, 1), slopes_ref[g * rep + r], F32) for r in range(rep)], axis=0)
    wl = WINDOW + tq

    @pl.when(i <= WINDOW // tq)
    def _():
        row = lax.broadcasted_iota(jnp.int32, (tq, 1), 0)
        dw = (jnp.concatenate([row] * rep, axis=0)
              + (jnp.minimum(s0, WINDOW) - lax.broadcasted_iota(jnp.int32, (1, wl), 1)))
        bw_ref[...] = jnp.where((dw >= 0) & (dw < WINDOW), -slope4 * dw.astype(F32), NEG_INF)

    kc = kc_ref[0, 0]
    ncp = kc.shape[0]
    sc = _dot_nt(q4, kc)
    cend = lax.broadcasted_iota(jnp.int32, (1, ncp), 1) * CMP_STRIDE + (CMP_BLOCK - 1)
    dist = t4 - cend
    vis = dist >= 0
    sc = jnp.where(vis, sc - slope4 * dist.astype(F32), NEG_INF)
    e = jnp.exp(sc - jnp.max(sc, axis=-1, keepdims=True))
    p = e * jnp.where(t4 >= CMP_BLOCK - 1, 1.0 / jnp.sum(e, axis=-1, keepdims=True), 0.0)
    o_c = _dot(p.astype(BF16), vc_ref[0, 0])

    psum = p[0:tq]
    for r in range(1, rep):
        psum = psum + p[r * tq:(r + 1) * tq]
    imp = _dot_split(psum, ov_ref[...])
    jf = lax.broadcasted_iota(jnp.int32, (tq, LANES), 1)
    qb = t // SEL_BLOCK
    forced = (jf == 0) | (jf == qb) | (jf == qb - 1)
    v = jnp.where(jf <= qb, imp + jnp.where(forced, FORCE_BONUS, 0.0), NEG_INF)
    jff = jf.astype(F32)
    selb = jnp.full((tq, LANES), NEG_INF, F32)
    for _ in range(SEL_TOPK):
        vmax = jnp.max(v, axis=-1, keepdims=True)
        first = jnp.min(jnp.where(v == vmax, jff, float(LANES)), axis=-1, keepdims=True)
        pick = jff == first
        selb = jnp.where(pick, 0.0, selb)
        v = jnp.where(pick, BELOW_NEG_INF, v)
    chosen = jnp.max(jnp.where(selb == 0.0, 1.0, 0.0), axis=0, keepdims=True)
    selb = selb.astype(BF16)

    qs = jnp.concatenate([qa, jnp.concatenate([selb] * rep, axis=0)], axis=1)

    def sel_tile(kt, masked):
        k0 = pl.multiple_of(kt * tk, tk)
        s = _dot_nt(qs, ks_ref[0, 0, pl.ds(k0, tk), :])
        if masked:
            s = jnp.where(t4 >= k0 + lax.broadcasted_iota(jnp.int32, (1, tk), 1), s, NEG_INF)
        delta = slope4 * (k0 - s0).astype(F32)
        m_ref[...], l_ref[...], acc_ref[...] = _flash_tile(
            s, delta, (m_ref[...], l_ref[...], acc_ref[...]), vs_ref[0, 0, pl.ds(k0, tk), :])

    m_ref[...] = jnp.full((rows, 1), NEG_INF, F32)
    l_ref[...] = jnp.zeros((rows, 1), F32)
    acc_ref[...] = jnp.zeros((rows, HEAD_DIM), F32)
    blk_lane = lax.broadcasted_iota(jnp.int32, (1, LANES), 1)
    blk_per_tile = tk // SEL_BLOCK

    def sel_full_tile(kt, carry):
        in_tile = (blk_lane >= kt * blk_per_tile) & (blk_lane < (kt + 1) * blk_per_tile)

        @pl.when(jnp.max(jnp.where(in_tile, chosen, 0.0)) > 0.0)
        def _():
            sel_tile(kt, False)

        return carry

    n_full = s0 // tk
    lax.fori_loop(0, n_full, sel_full_tile, 0)
    sel_tile(n_full, True)
    o_s = acc_ref[...] / l_ref[...]

    ws = pl.multiple_of(jnp.maximum(s0 - WINDOW, 0), tq)
    sw = _dot_nt(q4, kw_ref[0, 0, pl.ds(ws, wl), :]) + bw_ref[...]
    ew = jnp.exp(sw - jnp.max(sw, axis=-1, keepdims=True))
    o_w = _dot(ew.astype(BF16), vw_ref[0, 0, pl.ds(ws, wl), :]) / jnp.sum(ew, axis=-1, keepdims=True)

    gate = jax.nn.sigmoid(g_ref[0, 0].astype(F32))
    for r in range(rep):
        rr = slice(r * tq, (r + 1) * tq)
        o = (gate[:, 3 * r:3 * r + 1] * o_c[rr] + gate[:, 3 * r + 1:3 * r + 2] * o_s[rr]
             + gate[:, 3 * r + 2:3 * r + 3] * o_w[rr])
        o_ref[0, r] = o.astype(o_ref.dtype)


def _nsa(slopes, q, kc, vc, ks, vs, kw, vw, gates, ov, tq, tk):
    bsz, h, s, _ = q.shape
    g = NSA_KV_HEADS
    rep = NSA_REP
    dk = HEAD_DIM
    assert s % tk == 0 and tk % tq == 0 and tq % SEL_BLOCK == 0 and tk // POS_SPLIT <= 256
    assert s >= WINDOW + tq and WINDOW % tq == 0

    def resident(a):
        return pl.BlockSpec((1, 1) + a.shape[2:], lambda b, c, i: (b, c, 0, 0))

    return pl.pallas_call(
        functools.partial(_nsa_kernel, tq=tq, tk=tk),
        grid=(bsz, g, s // tq),
        in_specs=[pl.BlockSpec(memory_space=pltpu.SMEM),
                  pl.BlockSpec((1, rep, tq, AUG_W), lambda b, c, i: (b, c, i, 0)),
                  resident(kc), resident(vc), resident(ks), resident(vs), resident(kw), resident(vw),
                  pl.BlockSpec((1, 1, tq, 3 * rep), lambda b, c, i: (b, c, i, 0)),
                  pl.BlockSpec(ov.shape, lambda b, c, i: (0, 0))],
        out_specs=pl.BlockSpec((1, rep, tq, dk), lambda b, c, i: (b, c, i, 0)),
        out_shape=jax.ShapeDtypeStruct((bsz, h, s, dk), BF16),
        scratch_shapes=[pltpu.VMEM((rep * tq, WINDOW + tq), F32),
                        pltpu.VMEM((rep * tq, 1), F32), pltpu.VMEM((rep * tq, 1), F32),
                        pltpu.VMEM((rep * tq, dk), F32)],
        compiler_params=_params("parallel", "parallel", "arbitrary"),
    )(slopes, q, kc, vc, ks, vs, kw, vw, gates, ov)


def _diff_kernel(slopes_ref, q_ref, k_ref, v_ref, lam_ref, g_ref, o_ref, *, tq, tk, lam_init):
    h = pl.program_id(1)
    i = pl.program_id(2)
    s0 = i * tq
    slope = slopes_ref[h]
    t = s0 + lax.broadcasted_iota(jnp.int32, (tq, 1), 0)

    def tile(kt, carry, masked):
        k0 = pl.multiple_of(kt * tk, tk)
        delta = slope * (k0 - s0).astype(F32)
        v = v_ref[0, 0, pl.ds(k0, tk), :]
        new = []
        for mi in range(2):
            s = _dot_nt(q_ref[0, 0, mi], k_ref[0, 0, mi, pl.ds(k0, tk), :])
            if masked:
                s = jnp.where(t >= k0 + lax.broadcasted_iota(jnp.int32, (1, tk), 1), s, NEG_INF)
            new.append(_flash_tile(s, delta, carry[mi], v))
        return tuple(new)

    one = (jnp.full((tq, 1), NEG_INF, F32), jnp.zeros((tq, 1), F32),
           jnp.zeros((tq, DIFF_V_DIM), F32))
    n_full = s0 // tk
    carry = lax.fori_loop(0, n_full, functools.partial(tile, masked=False), (one, one))
    (_, l0, acc0), (_, l1, acc1) = tile(n_full, carry, True)
    lp = lam_ref[...]
    lam = (jnp.exp(jnp.sum(lp[0:1] * lp[1:2], axis=-1, keepdims=True))
           - jnp.exp(jnp.sum(lp[2:3] * lp[3:4], axis=-1, keepdims=True)) + lam_init)
    o = acc0 / l0 - lam * (acc1 / l1)
    o_ref[0] = (_rms(o, g_ref[...]) * (1.0 - lam_init)).astype(o_ref.dtype)


def _diff(slopes, q, k, v, lam_params, subln_g, tq, tk, lam_init):
    bsz, h, _, s, _ = q.shape
    dv = v.shape[-1]
    assert s % tk == 0 and tk % tq == 0 and tk // POS_SPLIT <= 256
    return pl.pallas_call(
        functools.partial(_diff_kernel, tq=tq, tk=tk, lam_init=lam_init),
        grid=(bsz, h, s // tq),
        in_specs=[pl.BlockSpec(memory_space=pltpu.SMEM),
                  pl.BlockSpec((1, 1, 2, tq, AUG_W), lambda b, c, i: (b, c, 0, i, 0)),
                  pl.BlockSpec((1, 1, 2, s, AUG_W), lambda b, c, i: (b, c, 0, 0, 0)),
                  pl.BlockSpec((1, 1, s, dv), lambda b, c, i: (b, c, 0, 0)),
                  pl.BlockSpec(lam_params.shape, lambda b, c, i: (0, 0)),
                  pl.BlockSpec((1, dv), lambda b, c, i: (0, 0))],
        out_specs=pl.BlockSpec((1, tq, dv), lambda b, c, i: (b, i, c)),
        out_shape=jax.ShapeDtypeStruct((bsz, s, h * dv), BF16),
        compiler_params=_params("parallel", "parallel", "arbitrary"),
    )(slopes, q, k, v, lam_params, subln_g.reshape(1, dv))


def _outproj_kernel(a_ref, b_ref, wo_ref, x_ref, g1_ref, sc_ref, sh_ref, ng1_ref, ng2_ref,
                    wr_ref, x1_ref, hb_ref, hrow_ref, lg_ref):
    half = a_ref.shape[-1]
    tm = a_ref.shape[1]
    y = _dot(a_ref[0], wo_ref[0:half, :]) + _dot(b_ref[0], wo_ref[half:, :])
    x1 = x_ref[0] + g1_ref[0] * _rms(y, ng1_ref[...])
    x1_ref[0] = x1
    h = _rms(x1, ng2_ref[...]) * (1.0 + sc_ref[0]) + sh_ref[0]
    hb_ref[0] = h.astype(BF16)
    for s in range(h.shape[1] // LANES):
        hrow_ref[pl.ds(s, tm, stride=SUBLANES), :] = h[:, s * LANES:(s + 1) * LANES]
    lg_ref[0] = _dot_f32(h, wr_ref[...])


def _outproj(a, b, wo, x, g1, sc2, sh2, ng1, ng2, wr, tm):
    bsz, s, d = x.shape
    half = a.shape[-1]
    ne = wr.shape[1]
    nt = s // tm
    vec = pl.BlockSpec((1, 1, d), lambda bb, i: (bb, 0, 0))
    row = pl.BlockSpec((1, d), lambda bb, i: (0, 0))
    tile = pl.BlockSpec((1, tm, d), lambda bb, i: (bb, i, 0))
    return pl.pallas_call(
        _outproj_kernel,
        grid=(bsz, nt),
        in_specs=[pl.BlockSpec((1, tm, half), lambda bb, i: (bb, i, 0)),
                  pl.BlockSpec((1, tm, half), lambda bb, i: (bb, i, 0)),
                  pl.BlockSpec((d, d), lambda bb, i: (0, 0)),
                  tile, vec, vec, vec, row, row,
                  pl.BlockSpec((d, ne), lambda bb, i: (0, 0))],
        out_specs=[tile, tile,
                   pl.BlockSpec((tm * d // LANES, LANES), lambda bb, i: (bb * nt + i, 0)),
                   pl.BlockSpec((1, tm, ne), lambda bb, i: (bb, i, 0))],
        out_shape=[jax.ShapeDtypeStruct((bsz, s, d), F32),
                   jax.ShapeDtypeStruct((bsz, s, d), BF16),
                   jax.ShapeDtypeStruct((bsz * s * d // LANES, LANES), F32),
                   jax.ShapeDtypeStruct((bsz, s, ne), F32)],
        compiler_params=_params("parallel", "parallel"),
    )(a, b, wo, x, g1, sc2, sh2, ng1, ng2, wr)


def _router_kernel(lg_ref, bias_ref, tri_ref, oi_ref, ow_ref, cnt_ref, carry_ref):
    @pl.when(pl.program_id(0) == 0)
    def _():
        carry_ref[...] = jnp.zeros_like(carry_ref)

    tm, ne = lg_ref.shape
    per_grp = ne // N_GROUPS
    aff = jax.nn.sigmoid(lg_ref[...])
    choice = aff + bias_ref[...]
    lane = lax.broadcasted_iota(jnp.int32, (tm, ne), 1)
    lanef = lane.astype(F32)
    grp = lane // per_grp

    def row_max(a):
        return jnp.max(a, axis=-1, keepdims=True)

    def first_lane(eq):
        return jnp.min(jnp.where(eq, lanef, float(ne)), axis=-1, keepdims=True)

    gscore = []
    for gi in range(N_GROUPS):
        mg = jnp.where(grp == gi, choice, BELOW_NEG_INF)
        m1 = row_max(mg)
        mg = jnp.where(lanef == first_lane(mg == m1), BELOW_NEG_INF, mg)
        gscore.append(m1 + row_max(mg))
    keep = jnp.zeros((tm, ne), F32)
    for gi in range(N_GROUPS):
        beaten = jnp.zeros((tm, 1), F32)
        for gj in range(N_GROUPS):
            if gj == gi:
                continue
            wins = gscore[gj] >= gscore[gi] if gj < gi else gscore[gj] > gscore[gi]
            beaten = beaten + jnp.where(wins, 1.0, 0.0)
        keep = jnp.where(grp == gi, jnp.where(beaten < TOPK_GROUPS, 1.0, 0.0), keep)
    ch = jnp.where(keep > 0.5, choice, NEG_INF)

    idx, wts = [], []
    onehot = jnp.zeros((tm, ne), F32)
    for _ in range(TOP_K):
        first = first_lane(ch == row_max(ch))
        pick = lanef == first
        wts.append(jnp.sum(jnp.where(pick, aff, 0.0), axis=-1, keepdims=True))
        ch = jnp.where(pick, BELOW_NEG_INF, ch)
        onehot = jnp.where(pick, 1.0, onehot)
        idx.append(first)
    wsum = wts[0]
    for w in wts[1:]:
        wsum = wsum + w
    before = carry_ref[...] + _dot(tri_ref[...], onehot.astype(BF16))
    carry_ref[...] = carry_ref[...] + jnp.sum(onehot, axis=0, keepdims=True)
    cnt_ref[...] = jnp.broadcast_to(carry_ref[...], cnt_ref.shape)

    l128 = lax.broadcasted_iota(jnp.int32, (tm, LANES), 1)
    oi = jnp.zeros((tm, LANES), jnp.int32)
    ow = jnp.zeros((tm, LANES), F32)
    for k in range(TOP_K):
        rank = jnp.sum(jnp.where(lanef == idx[k], before, 0.0), axis=-1, keepdims=True)
        oi = jnp.where(l128 == k, idx[k].astype(jnp.int32), oi)
        oi = jnp.where(l128 == TOP_K + k, rank.astype(jnp.int32), oi)
        ow = jnp.where(l128 == k, wts[k] / wsum * ROUTED_SCALE, ow)
    oi_ref[...] = oi
    ow_ref[...] = ow


def _router(logits, bias, tm):
    n_tok, ne = logits.shape
    tri = (jnp.arange(tm)[:, None] > jnp.arange(tm)[None, :]).astype(BF16)
    return pl.pallas_call(
        _router_kernel,
        grid=(n_tok // tm,),
        in_specs=[pl.BlockSpec((tm, ne), lambda i: (i, 0)),
                  pl.BlockSpec((1, ne), lambda i: (0, 0)),
                  pl.BlockSpec((tm, tm), lambda i: (0, 0))],
        out_specs=[pl.BlockSpec((tm, LANES), lambda i: (i, 0)),
                   pl.BlockSpec((tm, LANES), lambda i: (i, 0)),
                   pl.BlockSpec((SUBLANES, ne), lambda i: (0, 0))],
        out_shape=[jax.ShapeDtypeStruct((n_tok, LANES), jnp.int32),
                   jax.ShapeDtypeStruct((n_tok, LANES), F32),
                   jax.ShapeDtypeStruct((SUBLANES, ne), F32)],
        scratch_shapes=[pltpu.VMEM((1, ne), F32)],
        compiler_params=_params("arbitrary"),
    )(logits, bias.reshape(1, ne), tri)


def _row_copy(src, src_row, dst, dst_row, sem):
    return pltpu.make_async_copy(src.at[src_row], dst.at[dst_row], sem)


def _chunk_fill(zbuf, xs_ref, start, sem):
    return pltpu.make_async_copy(zbuf, xs_ref.at[pl.ds(start, EXPERT_CHUNK)], sem)


DISPATCH_SLOTS = 3
CHUNKS_PER_STEP = 8


def _dispatch_kernel(dest_ref, pstart_ref, pend_ref, h_hbm, xs_ref, zbuf, stage, sem, zsem,
                     lsem, *, th):
    step = pl.program_id(0)
    n_steps = pl.num_programs(0)

    @pl.when(step == 0)
    def _():
        zbuf[...] = jnp.zeros_like(zbuf)

        def fill(e, carry):
            @pl.when(pend_ref[e] > pstart_ref[e])
            def _():
                _chunk_fill(zbuf, xs_ref, pend_ref[e] - EXPERT_CHUNK, zsem).start()
            return carry

        lax.fori_loop(0, N_EXPERTS, fill, 0)

        def fill_done(e, carry):
            @pl.when(pend_ref[e] > pstart_ref[e])
            def _():
                _chunk_fill(zbuf, xs_ref, 0, zsem).wait()
            return carry

        lax.fori_loop(0, N_EXPERTS, fill_done, 0)

        first_unused = pend_ref[N_EXPERTS - 1] // EXPERT_CHUNK
        n_chunks = xs_ref.shape[0] // EXPERT_CHUNK

        def tail(ci, carry):
            _chunk_fill(zbuf, xs_ref, ci * EXPERT_CHUNK, zsem).start()
            return carry

        lax.fori_loop(first_unused, n_chunks, tail, 0)

        def tail_done(ci, carry):
            _chunk_fill(zbuf, xs_ref, 0, zsem).wait()
            return carry

        lax.fori_loop(first_unused, n_chunks, tail_done, 0)

    def load(s, slot):
        return pltpu.make_async_copy(h_hbm.at[pl.ds(s * th, th)], stage.at[slot], lsem.at[slot])

    @pl.when(step == 0)
    def _():
        load(0, 0).start()

    @pl.when(step + 1 < n_steps)
    def _():
        load(step + 1, (step + 1) % DISPATCH_SLOTS).start()

    slot = step % DISPATCH_SLOTS
    load(step, slot).wait()
    src = stage.at[slot]

    def issue(tl, carry):
        for k in range(TOP_K):
            a = tl * TOP_K + k
            _row_copy(src, tl, xs_ref, dest_ref[a], sem.at[step % 2]).start()
        return carry

    lax.fori_loop(0, th, issue, 0)

    def drain(parity):
        def body(tl, carry):
            for _ in range(TOP_K):
                _row_copy(src, 0, xs_ref, 0, sem.at[parity]).wait()
            return carry

        lax.fori_loop(0, th, body, 0)

    @pl.when(step > 0)
    def _():
        drain((step - 1) % 2)

    @pl.when(step == n_steps - 1)
    def _():
        drain(step % 2)


def _dispatch(dest_flat, pstart, pend, h_rows, n_rows, th):
    n_tok = h_rows.shape[0]
    row_shape = h_rows.shape[1:]
    smem_blk = pl.BlockSpec((th * TOP_K,), lambda i: (i,), memory_space=pltpu.SMEM)
    return pl.pallas_call(
        functools.partial(_dispatch_kernel, th=th),
        grid=(n_tok // th,),
        in_specs=[smem_blk,
                  pl.BlockSpec(memory_space=pltpu.SMEM),
                  pl.BlockSpec(memory_space=pltpu.SMEM),
                  pl.BlockSpec(memory_space=pl.ANY)],
        out_specs=pl.BlockSpec(memory_space=pl.ANY),
        out_shape=jax.ShapeDtypeStruct((n_rows,) + row_shape, h_rows.dtype),
        scratch_shapes=[pltpu.VMEM((EXPERT_CHUNK,) + row_shape, h_rows.dtype),
                        pltpu.VMEM((DISPATCH_SLOTS, th) + row_shape, h_rows.dtype),
                        pltpu.SemaphoreType.DMA((2,)), pltpu.SemaphoreType.DMA(()),
                        pltpu.SemaphoreType.DMA((DISPATCH_SLOTS,))],
        compiler_params=_params("arbitrary"),
    )(dest_flat, pstart, pend, h_rows)


def _weight_fetch(wgu_hbm, wdn_hbm, gu_buf, dn_buf, sems, expert, slot):
    return (pltpu.make_async_copy(wgu_hbm.at[expert], gu_buf.at[slot], sems.at[slot, 0]),
            pltpu.make_async_copy(wdn_hbm.at[expert], dn_buf.at[slot], sems.at[slot, 1]))


def _experts_kernel(ce_ref, first_ref, next_ref, slot_ref, na_ref, x_ref, wgu_hbm, wdn_hbm, o_ref,
                    gu_buf, dn_buf, gu_bf, dn_bf, sems):
    ch = EXPERT_CHUNK
    nsl = wgu_hbm.shape[1] // LANES
    chunk_rows = ch * nsl
    fetch = functools.partial(_weight_fetch, wgu_hbm, wdn_hbm, gu_buf, dn_buf, sems)

    @pl.when(pl.program_id(0) == 0)
    def _():
        for cp in fetch(ce_ref[0], 0):
            cp.start()

    for sub in range(CHUNKS_PER_STEP):
        i = pl.program_id(0) * CHUNKS_PER_STEP + sub
        active = i < na_ref[0]
        base = sub * chunk_rows

        @pl.when(active & (first_ref[i] == 1))
        def _():
            slot = slot_ref[i]
            for cp in fetch(ce_ref[i], slot):
                cp.wait()

            @pl.when(next_ref[i] >= 0)
            def _():
                for cp in fetch(next_ref[i], 1 - slot):
                    cp.start()

            gu_bf[...] = gu_buf[slot].astype(BF16)
            dn_bf[...] = dn_buf[slot].astype(BF16)

        @pl.when(active)
        def _():
            x = jnp.concatenate(
                [x_ref[pl.ds(base + s, ch, stride=SUBLANES), :] for s in range(nsl)],
                axis=1).astype(BF16)
            au = _dot(x, gu_bf[...])
            a = au[:, :EXPERT_DIM]
            u = au[:, EXPERT_DIM:]
            hmid = (a * jax.nn.sigmoid(a) * u).astype(BF16)
            y = _dot(hmid, dn_bf[...])
            for s in range(nsl):
                o_ref[pl.ds(base + s, ch, stride=SUBLANES), :] = y[:, s * LANES:(s + 1) * LANES]

        @pl.when(jnp.logical_not(active))
        def _():
            o_ref[pl.ds(base, chunk_rows), :] = jnp.zeros((chunk_rows, LANES), F32)


def _experts(chunk_e, first, next_e, slot, n_active, xs2d, w_gu, w_dn):
    n_chunks = chunk_e.shape[0]
    _, d, two_e = w_gu.shape
    blk_rows = CHUNKS_PER_STEP * EXPERT_CHUNK * d // LANES
    assert n_chunks % CHUNKS_PER_STEP == 0

    def x_map(i, ce, fs, nx, sl, na):
        return (jnp.minimum(i, (na[0] - 1) // CHUNKS_PER_STEP), 0)

    grid_spec = pltpu.PrefetchScalarGridSpec(
        num_scalar_prefetch=5,
        grid=(n_chunks // CHUNKS_PER_STEP,),
        in_specs=[pl.BlockSpec((blk_rows, LANES), x_map),
                  pl.BlockSpec(memory_space=pl.ANY),
                  pl.BlockSpec(memory_space=pl.ANY)],
        out_specs=pl.BlockSpec((blk_rows, LANES), lambda i, ce, fs, nx, sl, na: (i, 0)),
        scratch_shapes=[pltpu.VMEM((2, d, two_e), F32), pltpu.VMEM((2, two_e // 2, d), F32),
                        pltpu.VMEM((d, two_e), BF16), pltpu.VMEM((two_e // 2, d), BF16),
                        pltpu.SemaphoreType.DMA((2, 2))],
    )
    return pl.pallas_call(
        _experts_kernel,
        grid_spec=grid_spec,
        out_shape=jax.ShapeDtypeStruct(xs2d.shape, F32),
        compiler_params=_params("arbitrary"),
    )(chunk_e, first, next_e, slot, n_active, xs2d, w_gu, w_dn)


def _combine_kernel(dest_ref, dest_next_ref, w_ref, y_ref, o_ref, buf, sem, *, tj):
    step = pl.program_id(0)
    n_steps = pl.num_programs(0)
    cur = step % 2

    def gather(dest, slot):
        def issue(tl, carry):
            for k in range(TOP_K):
                a = tl * TOP_K + k
                _row_copy(y_ref, dest[a], buf.at[slot], a, sem.at[slot]).start()
            return carry

        lax.fori_loop(0, tj, issue, 0)

    @pl.when(step == 0)
    def _():
        gather(dest_ref, 0)

    @pl.when(step + 1 < n_steps)
    def _():
        gather(dest_next_ref, 1 - cur)

    rows = buf.at[cur]

    def drain(tl, carry):
        for _ in range(TOP_K):
            _row_copy(y_ref, 0, rows, 0, sem.at[cur]).wait()
        return carry

    lax.fori_loop(0, tj, drain, 0)

    def reduce(tl, carry):
        acc = w_ref[tl * TOP_K] * rows[tl * TOP_K]
        for k in range(1, TOP_K):
            acc = acc + w_ref[tl * TOP_K + k] * rows[tl * TOP_K + k]
        o_ref[tl] = acc
        return carry

    lax.fori_loop(0, tj, reduce, 0)


def _combine(dest_flat, w_flat, y_rows, n_tok, tj):
    row_shape = y_rows.shape[1:]
    n_steps = n_tok // tj
    smem_blk = pl.BlockSpec((tj * TOP_K,), lambda i: (i,), memory_space=pltpu.SMEM)
    smem_next = pl.BlockSpec((tj * TOP_K,), lambda i: (jnp.minimum(i + 1, n_steps - 1),),
                             memory_space=pltpu.SMEM)
    return pl.pallas_call(
        functools.partial(_combine_kernel, tj=tj),
        grid=(n_steps,),
        in_specs=[smem_blk, smem_next, smem_blk,
                  pl.BlockSpec(memory_space=pl.ANY)],
        out_specs=pl.BlockSpec((tj,) + row_shape, lambda i: (i, 0, 0)),
        out_shape=jax.ShapeDtypeStruct((n_tok,) + row_shape, F32),
        scratch_shapes=[pltpu.VMEM((2, tj * TOP_K) + row_shape, F32),
                        pltpu.SemaphoreType.DMA((2,))],
        compiler_params=_params("arbitrary"),
    )(dest_flat, dest_flat, w_flat, y_rows)


def _final_kernel(hb_ref, r_ref, x1_ref, g2_ref, ng_ref, wgu_ref, wdn_ref, o_ref):
    tm, d = hb_ref.shape[1:]
    au = _dot(hb_ref[0], wgu_ref[...])
    a = au[:, :SHARED_DIM]
    u = au[:, SHARED_DIM:]
    shared = _dot((a * jax.nn.sigmoid(a) * u).astype(BF16), wdn_ref[...])
    routed = jnp.concatenate(
        [r_ref[pl.ds(s, tm, stride=SUBLANES), :] for s in range(d // LANES)], axis=1)
    o_ref[0] = x1_ref[0] + g2_ref[0] * _rms(routed + shared, ng_ref[...])


def _final(hb, routed2d, x1, g2, ng, wgu, wdn, tm):
    bsz, s, d = x1.shape
    nt = s // tm
    tile = pl.BlockSpec((1, tm, d), lambda bb, i: (bb, i, 0))
    return pl.pallas_call(
        _final_kernel,
        grid=(bsz, nt),
        in_specs=[tile,
                  pl.BlockSpec((tm * d // LANES, LANES), lambda bb, i: (bb * nt + i, 0)),
                  tile,
                  pl.BlockSpec((1, 1, d), lambda bb, i: (bb, 0, 0)),
                  pl.BlockSpec((1, d), lambda bb, i: (0, 0)),
                  pl.BlockSpec(wgu.shape, lambda bb, i: (0, 0)),
                  pl.BlockSpec(wdn.shape, lambda bb, i: (0, 0))],
        out_specs=tile,
        out_shape=jax.ShapeDtypeStruct((bsz, s, d), F32),
        compiler_params=_params("parallel", "parallel"),
    )(hb, routed2d, x1, g2, ng, wgu, wdn)


def _alibi_slopes(n):
    return (2.0 ** (-8.0 * jnp.arange(1, n + 1, dtype=F32) / n)).astype(F32)


def _layer(x, c, w_ada, b_ada, norm_g, w_in, w_cmp1, w_cmp2, pe_cmp, diff_lambda, diff_subln,
           w_out, w_router, router_bias, w_gate_up, w_down, ws_gate_up, ws_down, layer):
    bsz, s, d = x.shape
    n_tok = bsz * s
    tm = 256
    tq = 256
    tk = min(KEY_TILE, s // 2)
    tk_nsa = min(NSA_KEY_TILE, s // 2)
    dk = HEAD_DIM
    g = NSA_KV_HEADS

    mod = _ada(c, w_ada, b_ada)
    sh1, sc1, g1, sh2, sc2, g2 = [m[:, None, :] for m in jnp.split(mod, 6, axis=-1)]

    n_pad = _round_up(IN_WIDTH, LANES)
    w_in_p = jnp.zeros((d, n_pad), BF16).at[:, :IN_WIDTH].set(w_in.astype(BF16))
    proj = _inproj(x, sc1, sh1, norm_g[0:1], w_in_p, tm)

    offs = [0]
    for wdt in IN_SPLITS:
        offs.append(offs[-1] + wdt)
    pieces = [proj[..., offs[j]:offs[j + 1]] for j in range(len(IN_SPLITS))]
    nq, kcm, vcm, ksel, vsel, kwin, vwin, ngate, dq, dkk, dv = pieces
    scale = dk ** -0.5

    def heads_first(a, nh):
        return a.reshape(bsz, s, nh, dk).transpose(0, 2, 1, 3)

    def aug_q(qh, slopes):
        shape = [1] * qh.ndim
        shape[1] = slopes.shape[0]
        lane = jnp.arange(AUG_W - dk)
        feat = jnp.where(lane == 0, POS_SPLIT, jnp.where(lane == 1, 1.0, 0.0)) * slopes[:, None]
        feat = feat.astype(BF16).reshape(shape[:-1] + [AUG_W - dk])
        return jnp.concatenate([qh, jnp.broadcast_to(feat, qh.shape[:-1] + (AUG_W - dk,))], axis=-1)

    def aug_k(kh, block_onehot, tile):
        col = (jnp.arange(s, dtype=jnp.int32) % tile)[:, None]
        lane = jnp.arange(AUG_W - dk)[None, :]
        feat = jnp.where(lane == 0, col // POS_SPLIT, jnp.where(lane == 1, col % POS_SPLIT, 0))
        parts = [kh, jnp.broadcast_to(feat.astype(BF16), kh.shape[:-1] + (AUG_W - dk,))]
        if block_onehot:
            oh = (jnp.arange(s)[:, None] // SEL_BLOCK == jnp.arange(LANES)[None, :]).astype(BF16)
            parts.append(jnp.broadcast_to(oh, kh.shape[:-1] + (LANES,)))
        return jnp.concatenate(parts, axis=-1)

    q_nsa = aug_q(heads_first(nq * scale, NSA_HEADS), _alibi_slopes(NSA_HEADS))
    ks = aug_k(heads_first(ksel, g), True, tk_nsa)
    vs, kw, vw = [heads_first(a, g) for a in (vsel, kwin, vwin)]
    gates = ngate.reshape(bsz, s, g, 3 * NSA_REP).transpose(0, 2, 1, 3)

    n_sub = CMP_BLOCK // CMP_STRIDE
    assert n_sub == 2
    ncp = s // CMP_STRIDE
    chunks = jnp.stack([heads_first(kcm, g), heads_first(vcm, g)]).reshape(
        2, bsz, g, ncp, CMP_STRIDE * dk)
    w1 = w_cmp1.astype(BF16).reshape(2, n_sub, CMP_STRIDE * dk, dk)
    pe8 = jnp.broadcast_to(pe_cmp.astype(BF16).reshape(2, 1, CMP_BLOCK * dk),
                           (2, SUBLANES, CMP_BLOCK * dk))
    cmp_kv = _compress(chunks, w1, w_cmp2.astype(BF16), pe8)

    n_blk = s // SEL_BLOCK
    assert n_blk <= LANES and min(SEL_TOPK, n_blk) == SEL_TOPK
    cs = jnp.arange(ncp)[:, None] * CMP_STRIDE
    bs = jnp.arange(LANES)[None, :] * SEL_BLOCK
    ov = (jnp.clip(jnp.minimum(cs + CMP_BLOCK, bs + SEL_BLOCK) - jnp.maximum(cs, bs), 0, None)
          .astype(F32) / CMP_BLOCK).astype(BF16)

    o_nsa = _nsa(_alibi_slopes(NSA_HEADS), q_nsa, cmp_kv[0], cmp_kv[1], ks, vs, kw, vw,
                 gates, ov, tq, tk_nsa)
    o_nsa = o_nsa.transpose(0, 2, 1, 3).reshape(bsz, s, NSA_Q_W)

    def maps_first(a):
        return a.reshape(bsz, s, DIFF_HEADS, 2, dk).transpose(0, 2, 3, 1, 4)

    lam_init = 0.8 - 0.6 * math.exp(-0.3 * layer)
    o_diff = _diff(_alibi_slopes(DIFF_HEADS),
                   aug_q(maps_first(dq * scale), _alibi_slopes(DIFF_HEADS)),
                   aug_k(maps_first(dkk), False, tk),
                   dv.reshape(bsz, s, DIFF_HEADS, DIFF_V_DIM).transpose(0, 2, 1, 3),
                   diff_lambda, diff_subln, min(DIFF_Q_TILE, tk), tk, lam_init)

    x1, hb, h_rows2d, logits = _outproj(o_nsa, o_diff, w_out.astype(BF16), x, g1, sc2, sh2,
                                        norm_g[1:2], norm_g[2:3], w_router, tm)

    oi, ow, cnt = _router(logits.reshape(n_tok, N_EXPERTS), router_bias, tm)
    idx_flat = oi[:, :TOP_K].reshape(-1)
    rank_flat = oi[:, TOP_K:2 * TOP_K].reshape(-1)
    w_flat = ow[:, :TOP_K].reshape(-1)
    counts = cnt[0].astype(jnp.int32)
    padded = (counts + EXPERT_CHUNK - 1) // EXPERT_CHUNK * EXPERT_CHUNK
    pend = jnp.cumsum(padded)
    pstart = (pend - padded).astype(jnp.int32)
    n_asg = n_tok * TOP_K
    n_chunks = (n_asg + N_EXPERTS * (EXPERT_CHUNK - 1) + EXPERT_CHUNK - 1) // EXPERT_CHUNK
    n_chunks = _round_up(n_chunks, CHUNKS_PER_STEP)
    n_rows = n_chunks * EXPERT_CHUNK
    chunk_start = jnp.arange(n_chunks, dtype=jnp.int32) * EXPERT_CHUNK
    chunk_e = jnp.minimum(
        jnp.sum((pend[None, :] <= chunk_start[:, None]).astype(jnp.int32), axis=1), N_EXPERTS - 1)
    n_active = (pend[-1:] // EXPERT_CHUNK).astype(jnp.int32)
    first = jnp.concatenate([jnp.ones((1,), jnp.int32),
                             (chunk_e[1:] != chunk_e[:-1]).astype(jnp.int32)])
    slot = (jnp.cumsum(first) - 1) % 2
    e_ids = jnp.arange(N_EXPERTS, dtype=jnp.int32)
    from_here = lax.cummin(jnp.where(counts > 0, e_ids, N_EXPERTS), axis=0, reverse=True)
    after = jnp.concatenate([from_here[1:], jnp.full((1,), N_EXPERTS, jnp.int32)])
    next_e = after[chunk_e]
    next_e = jnp.where(next_e == N_EXPERTS, -1, next_e).astype(jnp.int32)

    row_tile = (d // LANES, LANES)
    dest_flat = rank_flat + jnp.sum(
        jnp.where(idx_flat[:, None] == e_ids[None, :], pstart[None, :], 0), axis=1)
    xs = _dispatch(dest_flat, pstart, pend.astype(jnp.int32),
                   h_rows2d.reshape((n_tok,) + row_tile), n_rows, 128)
    ys = _experts(chunk_e, first, next_e, slot.astype(jnp.int32), n_active,
                  xs.reshape(n_rows * d // LANES, LANES), w_gate_up, w_down)
    routed = _combine(dest_flat, w_flat, ys.reshape((n_rows,) + row_tile), n_tok, 128)
    return _final(hb, routed.reshape(n_tok * d // LANES, LANES), x1, g2, norm_g[3:4],
                  ws_gate_up.astype(BF16), ws_down.astype(BF16), tm)


def kernel(x, c, w_ada, b_ada, norm_g, w_in, w_cmp1, w_cmp2, pe_cmp, diff_lambda, diff_subln,
           w_out, w_router, router_bias, w_gate_up, w_down, ws_gate_up, ws_down):
    for layer in range(w_ada.shape[0]):
        x = _layer(x, c, w_ada[layer], b_ada[layer], norm_g[layer], w_in[layer], w_cmp1[layer],
                   w_cmp2[layer], pe_cmp[layer], diff_lambda[layer], diff_subln[layer],
                   w_out[layer], w_router[layer], router_bias[layer], w_gate_up[layer],
                   w_down[layer], ws_gate_up[layer], ws_down[layer], layer)
    return x
```

```python
import functools
import math

import jax
import jax.numpy as jnp
from jax import lax
from jax.experimental import pallas as pl
from jax.experimental.pallas import tpu as pltpu

HEAD_DIM = 64
NSA_HEADS = 8
NSA_KV_HEADS = 2
NSA_REP = NSA_HEADS // NSA_KV_HEADS
CMP_BLOCK = 32
CMP_STRIDE = 16
SEL_BLOCK = 64
SEL_TOPK = 16
WINDOW = 512
DIFF_HEADS = 4
DIFF_V_DIM = 2 * HEAD_DIM
N_EXPERTS = 256
TOP_K = 8
N_GROUPS = 8
TOPK_GROUPS = 4
EXPERT_DIM = 256
SHARED_DIM = 256
ROUTED_SCALE = 2.5
EXPERT_CHUNK = 128
RMS_EPS = 1e-6
NEG_INF = -1e30
BELOW_NEG_INF = -3e38
FORCE_BONUS = 1e4

NSA_Q_W = NSA_HEADS * HEAD_DIM
NSA_KV_W = NSA_KV_HEADS * HEAD_DIM
NSA_GATE_W = 3 * NSA_HEADS
DIFF_QK_W = DIFF_HEADS * 2 * HEAD_DIM
DIFF_V_W = DIFF_HEADS * DIFF_V_DIM
IN_SPLITS = (NSA_Q_W,) + (NSA_KV_W,) * 6 + (NSA_GATE_W, DIFF_QK_W, DIFF_QK_W, DIFF_V_W)
IN_WIDTH = sum(IN_SPLITS)

LANES = 128
SUBLANES = 8
VMEM_LIMIT = 56 * 1024 * 1024
KEY_TILE = 1024
DIFF_Q_TILE = 512
NSA_KEY_TILE = 1024

F32 = jnp.float32
BF16 = jnp.bfloat16


def _round_up(a, m):
    return (a + m - 1) // m * m


def _dot(a, b):
    return jnp.dot(a, b, preferred_element_type=F32)


def _dot_nt(a, b):
    return lax.dot_general(a, b, (((1,), (1,)), ((), ())), preferred_element_type=F32)


def _dot_split(a, b_bf16):
    hi = a.astype(BF16)
    lo = (a - hi.astype(F32)).astype(BF16)
    return _dot(hi, b_bf16) + _dot(lo, b_bf16)


def _dot_f32(a, b):
    a_hi = a.astype(BF16)
    a_lo = (a - a_hi.astype(F32)).astype(BF16)
    b_hi = b.astype(BF16)
    b_lo = (b - b_hi.astype(F32)).astype(BF16)
    return _dot(a_hi, b_hi) + (_dot(a_lo, b_hi) + _dot(a_hi, b_lo))


def _rms(x, g):
    return x * lax.rsqrt(jnp.mean(x * x, axis=-1, keepdims=True) + RMS_EPS) * g


def _params(*sem):
    return pltpu.CompilerParams(dimension_semantics=sem, vmem_limit_bytes=VMEM_LIMIT)


def _ada_kernel(c_ref, w_ref, b_ref, o_ref):
    c = c_ref[...]
    h = c * jax.nn.sigmoid(c)
    o_ref[...] = _dot_f32(h, w_ref[...]) + b_ref[...]


def _ada(c, w, b):
    bsz, d = c.shape
    n = w.shape[1]
    tn = n // 4
    cp = jnp.zeros((SUBLANES, d), F32).at[:bsz].set(c)
    out = pl.pallas_call(
        _ada_kernel,
        grid=(n // tn,),
        in_specs=[pl.BlockSpec((SUBLANES, d), lambda j: (0, 0)),
                  pl.BlockSpec((d, tn), lambda j: (0, j)),
                  pl.BlockSpec((1, tn), lambda j: (0, j))],
        out_specs=pl.BlockSpec((SUBLANES, tn), lambda j: (0, j)),
        out_shape=jax.ShapeDtypeStruct((SUBLANES, n), F32),
        compiler_params=_params("arbitrary"),
    )(cp, w, b.reshape(1, n))
    return out[:bsz]


def _inproj_kernel(x_ref, sc_ref, sh_ref, g_ref, w_ref, o_ref):
    h = _rms(x_ref[0], g_ref[...]) * (1.0 + sc_ref[0]) + sh_ref[0]
    o_ref[0] = _dot(h.astype(BF16), w_ref[...]).astype(o_ref.dtype)


def _inproj(x, sc, sh, g, w_bf16, tm):
    bsz, s, d = x.shape
    n = w_bf16.shape[1]
    return pl.pallas_call(
        _inproj_kernel,
        grid=(bsz, s // tm),
        in_specs=[pl.BlockSpec((1, tm, d), lambda b, i: (b, i, 0)),
                  pl.BlockSpec((1, 1, d), lambda b, i: (b, 0, 0)),
                  pl.BlockSpec((1, 1, d), lambda b, i: (b, 0, 0)),
                  pl.BlockSpec((1, d), lambda b, i: (0, 0)),
                  pl.BlockSpec((d, n), lambda b, i: (0, 0))],
        out_specs=pl.BlockSpec((1, tm, n), lambda b, i: (b, i, 0)),
        out_shape=jax.ShapeDtypeStruct((bsz, s, n), BF16),
        compiler_params=_params("parallel", "parallel"),
    )(x, sc, sh, g, w_bf16)


def _cmp_kernel(ch_ref, w1_ref, w2_ref, pe_ref, o_ref):
    ch = ch_ref[0, 0, 0]
    half = ch.shape[1]
    a = _dot(ch, w1_ref[0, 0])
    bm = _dot(ch, w1_ref[0, 1])
    n = bm.shape[0]
    bm_next = pltpu.roll(bm, shift=n - 1, axis=0)
    pe = pe_ref[0]
    pt = _dot(pe[:, :half], w1_ref[0, 0]) + _dot(pe[:, half:], w1_ref[0, 1])
    hmid = jax.nn.gelu(a + bm_next + pt[0:1])
    o_ref[0, 0, 0] = _dot(hmid.astype(BF16), w2_ref[0]).astype(o_ref.dtype)


def _compress(chunks, w1, w2, pe8):
    two, bsz, g, ncp, cw = chunks.shape
    dk = w2.shape[-1]
    return pl.pallas_call(
        _cmp_kernel,
        grid=(two, bsz, g),
        in_specs=[pl.BlockSpec((1, 1, 1, ncp, cw), lambda a, b, c: (a, b, c, 0, 0)),
                  pl.BlockSpec((1, 2, cw, dk), lambda a, b, c: (a, 0, 0, 0)),
                  pl.BlockSpec((1, dk, dk), lambda a, b, c: (a, 0, 0)),
                  pl.BlockSpec((1, SUBLANES, 2 * cw), lambda a, b, c: (a, 0, 0))],
        out_specs=pl.BlockSpec((1, 1, 1, ncp, dk), lambda a, b, c: (a, b, c, 0, 0)),
        out_shape=jax.ShapeDtypeStruct((two, bsz, g, ncp, dk), BF16),
        compiler_params=_params("parallel", "parallel", "parallel"),
    )(chunks, w1, w2, pe8)


AUG_W = LANES
POS_SPLIT = 64


def _flash_tile(s, delta, carry, v):
    m, l, acc = carry
    m_new = jnp.maximum(m, jnp.max(s, axis=-1, keepdims=True) + delta)
    pr = jnp.exp(s - (m_new - delta))
    alpha = jnp.exp(m - m_new)
    return (m_new, alpha * l + jnp.sum(pr, axis=-1, keepdims=True),
            alpha * acc + _dot(pr.astype(BF16), v))


def _nsa_kernel(slopes_ref, q_ref, kc_ref, vc_ref, ks_ref, vs_ref, kw_ref, vw_ref,
                g_ref, ov_ref, o_ref, bw_ref, m_ref, acc_ref, *, tq, tk):
    g = pl.program_id(1)
    i = pl.program_id(2)
    rep = NSA_REP
    rows = rep * tq
    s0 = i * tq
    qa = q_ref[0].reshape(rows, AUG_W)
    q4 = qa[:, :HEAD_DIM]
    t = s0 + lax.broadcasted_iota(jnp.int32, (tq, 1), 0)
    t4 = jnp.concatenate([t] * rep, axis=0)
    slope4 = jnp.concatenate(
        [jnp.full((tq, 1), slopes_ref[g * rep + r], F32) for r in range(rep)], axis=0)
    wl = WINDOW + tq

    @pl.when(i <= WINDOW // tq)
    def _():
        row = lax.broadcasted_iota(jnp.int32, (tq, 1), 0)
        dw = (jnp.concatenate([row] * rep, axis=0)
              + (jnp.minimum(s0, WINDOW) - lax.broadcasted_iota(jnp.int32, (1, wl), 1)))
        bw_ref[...] = jnp.where((dw >= 0) & (dw < WINDOW), -slope4 * dw.astype(F32), NEG_INF)

    kc = kc_ref[0, 0]
    ncp = kc.shape[0]
    sc = _dot_nt(q4, kc)
    cend = lax.broadcasted_iota(jnp.int32, (1, ncp), 1) * CMP_STRIDE + (CMP_BLOCK - 1)
    dist = t4 - cend
    vis = dist >= 0
    sc = jnp.where(vis, sc - slope4 * dist.astype(F32), NEG_INF)
    e = jnp.exp(sc - jnp.max(sc, axis=-1, keepdims=True))
    p = e * jnp.where(t4 >= CMP_BLOCK - 1, 1.0 / jnp.sum(e, axis=-1, keepdims=True), 0.0)
    o_c = _dot(p.astype(BF16), vc_ref[0, 0])

    psum = p[0:tq]
    for r in range(1, rep):
        psum = psum + p[r * tq:(r + 1) * tq]
    imp = _dot_split(psum, ov_ref[...])
    jf = lax.broadcasted_iota(jnp.int32, (tq, LANES), 1)
    qb = t // SEL_BLOCK
    forced = (jf == 0) | (jf == qb) | (jf == qb - 1)
    v = jnp.where(jf <= qb, imp + jnp.where(forced, FORCE_BONUS, 0.0), NEG_INF)
    jff = jf.astype(F32)
    selb = jnp.full((tq, LANES), NEG_INF, F32)
    for _ in range(SEL_TOPK):
        vmax = jnp.max(v, axis=-1, keepdims=True)
        first = jnp.min(jnp.where(v == vmax, jff, float(LANES)), axis=-1, keepdims=True)
        pick = jff == first
        selb = jnp.where(pick, 0.0, selb)
        v = jnp.where(pick, BELOW_NEG_INF, v)
    chosen = jnp.max(jnp.where(selb == 0.0, 1.0, 0.0), axis=0, keepdims=True)
    selb = selb.astype(BF16)

    qs = jnp.concatenate([qa, jnp.concatenate([selb] * rep, axis=0)], axis=1)

    def sel_tile(kt, masked):
        k0 = pl.multiple_of(kt * tk, tk)
        s = _dot_nt(qs, ks_ref[0, 0, pl.ds(k0, tk), :])
        if masked:
            s = jnp.where(t4 >= k0 + lax.broadcasted_iota(jnp.int32, (1, tk), 1), s, NEG_INF)
        delta = slope4 * (k0 - s0).astype(F32)
        m = m_ref[...]
        m_new = jnp.maximum(m, jnp.max(s, axis=-1, keepdims=True) + delta)
        pr = jnp.exp(s - (m_new - delta)).astype(BF16)
        acc_ref[...] = jnp.exp(m - m_new) * acc_ref[...] + _dot(pr, vs_ref[0, 0, pl.ds(k0, tk), :])
        m_ref[...] = m_new

    m_ref[...] = jnp.full((rows, 1), NEG_INF, F32)
    acc_ref[...] = jnp.zeros((rows, AUG_W), F32)
    blk_lane = lax.broadcasted_iota(jnp.int32, (1, LANES), 1)
    blk_per_tile = tk // SEL_BLOCK

    def sel_full_tile(kt, carry):
        in_tile = (blk_lane >= kt * blk_per_tile) & (blk_lane < (kt + 1) * blk_per_tile)

        @pl.when(jnp.max(jnp.where(in_tile, chosen, 0.0)) > 0.0)
        def _():
            sel_tile(kt, False)

        return carry

    n_full = s0 // tk
    lax.fori_loop(0, n_full, sel_full_tile, 0)
    sel_tile(n_full, True)
    acc_s = acc_ref[...]
    o_s = acc_s[:, :HEAD_DIM] / acc_s[:, HEAD_DIM:HEAD_DIM + 1]

    ws = pl.multiple_of(jnp.maximum(s0 - WINDOW, 0), tq)
    sw = _dot_nt(q4, kw_ref[0, 0, pl.ds(ws, wl), :]) + bw_ref[...]
    ew = jnp.exp(sw - jnp.max(sw, axis=-1, keepdims=True))
    o_w = _dot(ew.astype(BF16), vw_ref[0, 0, pl.ds(ws, wl), :]) / jnp.sum(ew, axis=-1, keepdims=True)

    gate = jax.nn.sigmoid(g_ref[0, 0].astype(F32))
    for r in range(rep):
        rr = slice(r * tq, (r + 1) * tq)
        o = (gate[:, 3 * r:3 * r + 1] * o_c[rr] + gate[:, 3 * r + 1:3 * r + 2] * o_s[rr]
             + gate[:, 3 * r + 2:3 * r + 3] * o_w[rr])
        o_ref[0, r] = o.astype(o_ref.dtype)


def _nsa(slopes, q, kc, vc, ks, vs, kw, vw, gates, ov, tq, tk):
    bsz, h, s, _ = q.shape
    g = NSA_KV_HEADS
    rep = NSA_REP
    dk = HEAD_DIM
    assert s % tk == 0 and tk % tq == 0 and tq % SEL_BLOCK == 0 and tk // POS_SPLIT <= 256
    assert s >= WINDOW + tq and WINDOW % tq == 0

    def resident(a):
        return pl.BlockSpec((1, 1) + a.shape[2:], lambda b, c, i: (b, c, 0, 0))

    return pl.pallas_call(
        functools.partial(_nsa_kernel, tq=tq, tk=tk),
        grid=(bsz, g, s // tq),
        in_specs=[pl.BlockSpec(memory_space=pltpu.SMEM),
                  pl.BlockSpec((1, rep, tq, AUG_W), lambda b, c, i: (b, c, i, 0)),
                  resident(kc), resident(vc), resident(ks), resident(vs), resident(kw), resident(vw),
                  pl.BlockSpec((1, 1, tq, 3 * rep), lambda b, c, i: (b, c, i, 0)),
                  pl.BlockSpec(ov.shape, lambda b, c, i: (0, 0))],
        out_specs=pl.BlockSpec((1, rep, tq, dk), lambda b, c, i: (b, c, i, 0)),
        out_shape=jax.ShapeDtypeStruct((bsz, h, s, dk), BF16),
        scratch_shapes=[pltpu.VMEM((rep * tq, WINDOW + tq), F32),
                        pltpu.VMEM((rep * tq, 1), F32), pltpu.VMEM((rep * tq, AUG_W), F32)],
        compiler_params=_params("parallel", "parallel", "arbitrary"),
    )(slopes, q, kc, vc, ks, vs, kw, vw, gates, ov)


def _diff_kernel(slopes_ref, q_ref, k_ref, v_ref, lam_ref, g_ref, o_ref, *, tq, tk, lam_init):
    h = pl.program_id(1)
    i = pl.program_id(2)
    s0 = i * tq
    slope = slopes_ref[h]
    t = s0 + lax.broadcasted_iota(jnp.int32, (tq, 1), 0)

    def tile(kt, carry, masked):
        k0 = pl.multiple_of(kt * tk, tk)
        delta = slope * (k0 - s0).astype(F32)
        v = v_ref[0, 0, pl.ds(k0, tk), :]
        new = []
        for mi in range(2):
            s = _dot_nt(q_ref[0, 0, mi], k_ref[0, 0, mi, pl.ds(k0, tk), :])
            if masked:
                s = jnp.where(t >= k0 + lax.broadcasted_iota(jnp.int32, (1, tk), 1), s, NEG_INF)
            new.append(_flash_tile(s, delta, carry[mi], v))
        return tuple(new)

    one = (jnp.full((tq, 1), NEG_INF, F32), jnp.zeros((tq, 1), F32),
           jnp.zeros((tq, DIFF_V_DIM), F32))
    n_full = s0 // tk
    carry = lax.fori_loop(0, n_full, functools.partial(tile, masked=False), (one, one))
    (_, l0, acc0), (_, l1, acc1) = tile(n_full, carry, True)
    lp = lam_ref[...]
    lam = (jnp.exp(jnp.sum(lp[0:1] * lp[1:2], axis=-1, keepdims=True))
           - jnp.exp(jnp.sum(lp[2:3] * lp[3:4], axis=-1, keepdims=True)) + lam_init)
    o = acc0 / l0 - lam * (acc1 / l1)
    o_ref[0] = (_rms(o, g_ref[...]) * (1.0 - lam_init)).astype(o_ref.dtype)


def _diff(slopes, q, k, v, lam_params, subln_g, tq, tk, lam_init):
    bsz, h, _, s, _ = q.shape
    dv = v.shape[-1]
    assert s % tk == 0 and tk % tq == 0 and tk // POS_SPLIT <= 256
    return pl.pallas_call(
        functools.partial(_diff_kernel, tq=tq, tk=tk, lam_init=lam_init),
        grid=(bsz, h, s // tq),
        in_specs=[pl.BlockSpec(memory_space=pltpu.SMEM),
                  pl.BlockSpec((1, 1, 2, tq, AUG_W), lambda b, c, i: (b, c, 0, i, 0)),
                  pl.BlockSpec((1, 1, 2, s, AUG_W), lambda b, c, i: (b, c, 0, 0, 0)),
                  pl.BlockSpec((1, 1, s, dv), lambda b, c, i: (b, c, 0, 0)),
                  pl.BlockSpec(lam_params.shape, lambda b, c, i: (0, 0)),
                  pl.BlockSpec((1, dv), lambda b, c, i: (0, 0))],
        out_specs=pl.BlockSpec((1, tq, dv), lambda b, c, i: (b, i, c)),
        out_shape=jax.ShapeDtypeStruct((bsz, s, h * dv), BF16),
        compiler_params=_params("parallel", "parallel", "arbitrary"),
    )(slopes, q, k, v, lam_params, subln_g.reshape(1, dv))


def _outproj_kernel(a_ref, b_ref, wo_ref, x_ref, g1_ref, sc_ref, sh_ref, ng1_ref, ng2_ref,
                    wr_ref, x1_ref, hb_ref, hrow_ref, lg_ref):
    half = a_ref.shape[-1]
    tm = a_ref.shape[1]
    y = _dot(a_ref[0], wo_ref[0:half, :]) + _dot(b_ref[0], wo_ref[half:, :])
    x1 = x_ref[0] + g1_ref[0] * _rms(y, ng1_ref[...])
    x1_ref[0] = x1
    h = _rms(x1, ng2_ref[...]) * (1.0 + sc_ref[0]) + sh_ref[0]
    hb_ref[0] = h.astype(BF16)
    for s in range(h.shape[1] // LANES):
        hrow_ref[pl.ds(s, tm, stride=SUBLANES), :] = h[:, s * LANES:(s + 1) * LANES]
    lg_ref[0] = _dot_f32(h, wr_ref[...])


def _outproj(a, b, wo, x, g1, sc2, sh2, ng1, ng2, wr, tm):
    bsz, s, d = x.shape
    half = a.shape[-1]
    ne = wr.shape[1]
    nt = s // tm
    vec = pl.BlockSpec((1, 1, d), lambda bb, i: (bb, 0, 0))
    row = pl.BlockSpec((1, d), lambda bb, i: (0, 0))
    tile = pl.BlockSpec((1, tm, d), lambda bb, i: (bb, i, 0))
    return pl.pallas_call(
        _outproj_kernel,
        grid=(bsz, nt),
        in_specs=[pl.BlockSpec((1, tm, half), lambda bb, i: (bb, i, 0)),
                  pl.BlockSpec((1, tm, half), lambda bb, i: (bb, i, 0)),
                  pl.BlockSpec((d, d), lambda bb, i: (0, 0)),
                  tile, vec, vec, vec, row, row,
                  pl.BlockSpec((d, ne), lambda bb, i: (0, 0))],
        out_specs=[tile, tile,
                   pl.BlockSpec((tm * d // LANES, LANES), lambda bb, i: (bb * nt + i, 0)),
                   pl.BlockSpec((1, tm, ne), lambda bb, i: (bb, i, 0))],
        out_shape=[jax.ShapeDtypeStruct((bsz, s, d), F32),
                   jax.ShapeDtypeStruct((bsz, s, d), BF16),
                   jax.ShapeDtypeStruct((bsz * s * d // LANES, LANES), F32),
                   jax.ShapeDtypeStruct((bsz, s, ne), F32)],
        compiler_params=_params("parallel", "parallel"),
    )(a, b, wo, x, g1, sc2, sh2, ng1, ng2, wr)


def _router_kernel(lg_ref, bias_ref, tri_ref, oi_ref, ow_ref, cnt_ref, carry_ref):
    @pl.when(pl.program_id(0) == 0)
    def _():
        carry_ref[...] = jnp.zeros_like(carry_ref)

    tm, ne = lg_ref.shape
    per_grp = ne // N_GROUPS
    aff = jax.nn.sigmoid(lg_ref[...])
    choice = aff + bias_ref[...]
    lane = lax.broadcasted_iota(jnp.int32, (tm, ne), 1)
    lanef = lane.astype(F32)
    grp = lane // per_grp

    def row_max(a):
        return jnp.max(a, axis=-1, keepdims=True)

    def first_lane(eq):
        return jnp.min(jnp.where(eq, lanef, float(ne)), axis=-1, keepdims=True)

    gscore = []
    for gi in range(N_GROUPS):
        mg = jnp.where(grp == gi, choice, BELOW_NEG_INF)
        m1 = row_max(mg)
        mg = jnp.where(lanef == first_lane(mg == m1), BELOW_NEG_INF, mg)
        gscore.append(m1 + row_max(mg))
    keep = jnp.zeros((tm, ne), F32)
    for gi in range(N_GROUPS):
        beaten = jnp.zeros((tm, 1), F32)
        for gj in range(N_GROUPS):
            if gj == gi:
                continue
            wins = gscore[gj] >= gscore[gi] if gj < gi else gscore[gj] > gscore[gi]
            beaten = beaten + jnp.where(wins, 1.0, 0.0)
        keep = jnp.where(grp == gi, jnp.where(beaten < TOPK_GROUPS, 1.0, 0.0), keep)
    ch = jnp.where(keep > 0.5, choice, NEG_INF)

    idx, wts = [], []
    onehot = jnp.zeros((tm, ne), F32)
    for _ in range(TOP_K):
        first = first_lane(ch == row_max(ch))
        pick = lanef == first
        wts.append(jnp.sum(jnp.where(pick, aff, 0.0), axis=-1, keepdims=True))
        ch = jnp.where(pick, BELOW_NEG_INF, ch)
        onehot = jnp.where(pick, 1.0, onehot)
        idx.append(first)
    wsum = wts[0]
    for w in wts[1:]:
        wsum = wsum + w
    before = carry_ref[...] + _dot(tri_ref[...], onehot.astype(BF16))
    carry_ref[...] = carry_ref[...] + jnp.sum(onehot, axis=0, keepdims=True)
    cnt_ref[...] = jnp.broadcast_to(carry_ref[...], cnt_ref.shape)

    l128 = lax.broadcasted_iota(jnp.int32, (tm, LANES), 1)
    oi = jnp.zeros((tm, LANES), jnp.int32)
    ow = jnp.zeros((tm, LANES), F32)
    for k in range(TOP_K):
        rank = jnp.sum(jnp.where(lanef == idx[k], before, 0.0), axis=-1, keepdims=True)
        oi = jnp.where(l128 == k, idx[k].astype(jnp.int32), oi)
        oi = jnp.where(l128 == TOP_K + k, rank.astype(jnp.int32), oi)
        ow = jnp.where(l128 == k, wts[k] / wsum * ROUTED_SCALE, ow)
    oi_ref[...] = oi
    ow_ref[...] = ow


def _router(logits, bias, tm):
    n_tok, ne = logits.shape
    tri = (jnp.arange(tm)[:, None] > jnp.arange(tm)[None, :]).astype(BF16)
    return pl.pallas_call(
        _router_kernel,
        grid=(n_tok // tm,),
        in_specs=[pl.BlockSpec((tm, ne), lambda i: (i, 0)),
                  pl.BlockSpec((1, ne), lambda i: (0, 0)),
                  pl.BlockSpec((tm, tm), lambda i: (0, 0))],
        out_specs=[pl.BlockSpec((tm, LANES), lambda i: (i, 0)),
                   pl.BlockSpec((tm, LANES), lambda i: (i, 0)),
                   pl.BlockSpec((SUBLANES, ne), lambda i: (0, 0))],
        out_shape=[jax.ShapeDtypeStruct((n_tok, LANES), jnp.int32),
                   jax.ShapeDtypeStruct((n_tok, LANES), F32),
                   jax.ShapeDtypeStruct((SUBLANES, ne), F32)],
        scratch_shapes=[pltpu.VMEM((1, ne), F32)],
        compiler_params=_params("arbitrary"),
    )(logits, bias.reshape(1, ne), tri)


def _row_copy(src, src_row, dst, dst_row, sem):
    return pltpu.make_async_copy(src.at[src_row], dst.at[dst_row], sem)


def _chunk_fill(zbuf, xs_ref, start, sem):
    return pltpu.make_async_copy(zbuf, xs_ref.at[pl.ds(start, EXPERT_CHUNK)], sem)


DISPATCH_SLOTS = 3
CHUNKS_PER_STEP = 4


def _dispatch_kernel(dest_ref, pstart_ref, pend_ref, h_hbm, xs_ref, zbuf, stage, sem, zsem,
                     lsem, *, th):
    step = pl.program_id(0)
    n_steps = pl.num_programs(0)

    @pl.when(step == 0)
    def _():
        zbuf[...] = jnp.zeros_like(zbuf)

        def fill(e, carry):
            @pl.when(pend_ref[e] > pstart_ref[e])
            def _():
                _chunk_fill(zbuf, xs_ref, pend_ref[e] - EXPERT_CHUNK, zsem).start()
            return carry

        lax.fori_loop(0, N_EXPERTS, fill, 0)

        def fill_done(e, carry):
            @pl.when(pend_ref[e] > pstart_ref[e])
            def _():
                _chunk_fill(zbuf, xs_ref, 0, zsem).wait()
            return carry

        lax.fori_loop(0, N_EXPERTS, fill_done, 0)

        first_unused = pend_ref[N_EXPERTS - 1] // EXPERT_CHUNK
        n_chunks = xs_ref.shape[0] // EXPERT_CHUNK

        def tail(ci, carry):
            _chunk_fill(zbuf, xs_ref, ci * EXPERT_CHUNK, zsem).start()
            return carry

        lax.fori_loop(first_unused, n_chunks, tail, 0)

        def tail_done(ci, carry):
            _chunk_fill(zbuf, xs_ref, 0, zsem).wait()
            return carry

        lax.fori_loop(first_unused, n_chunks, tail_done, 0)

    def load(s, slot):
        return pltpu.make_async_copy(h_hbm.at[pl.ds(s * th, th)], stage.at[slot], lsem.at[slot])

    @pl.when(step == 0)
    def _():
        load(0, 0).start()

    @pl.when(step + 1 < n_steps)
    def _():
        load(step + 1, (step + 1) % DISPATCH_SLOTS).start()

    slot = step % DISPATCH_SLOTS
    load(step, slot).wait()
    src = stage.at[slot]

    def issue(tl, carry):
        for k in range(TOP_K):
            a = tl * TOP_K + k
            _row_copy(src, tl, xs_ref, dest_ref[a], sem.at[step % 2]).start()
        return carry

    lax.fori_loop(0, th, issue, 0)

    def drain(parity):
        def body(tl, carry):
            for _ in range(TOP_K):
                _row_copy(src, 0, xs_ref, 0, sem.at[parity]).wait()
            return carry

        lax.fori_loop(0, th, body, 0)

    @pl.when(step > 0)
    def _():
        drain((step - 1) % 2)

    @pl.when(step == n_steps - 1)
    def _():
        drain(step % 2)


def _dispatch(dest_flat, pstart, pend, h_rows, n_rows, th):
    n_tok = h_rows.shape[0]
    row_shape = h_rows.shape[1:]
    smem_blk = pl.BlockSpec((th * TOP_K,), lambda i: (i,), memory_space=pltpu.SMEM)
    return pl.pallas_call(
        functools.partial(_dispatch_kernel, th=th),
        grid=(n_tok // th,),
        in_specs=[smem_blk,
                  pl.BlockSpec(memory_space=pltpu.SMEM),
                  pl.BlockSpec(memory_space=pltpu.SMEM),
                  pl.BlockSpec(memory_space=pl.ANY)],
        out_specs=pl.BlockSpec(memory_space=pl.ANY),
        out_shape=jax.ShapeDtypeStruct((n_rows,) + row_shape, h_rows.dtype),
        scratch_shapes=[pltpu.VMEM((EXPERT_CHUNK,) + row_shape, h_rows.dtype),
                        pltpu.VMEM((DISPATCH_SLOTS, th) + row_shape, h_rows.dtype),
                        pltpu.SemaphoreType.DMA((2,)), pltpu.SemaphoreType.DMA(()),
                        pltpu.SemaphoreType.DMA((DISPATCH_SLOTS,))],
        compiler_params=_params("arbitrary"),
    )(dest_flat, pstart, pend, h_rows)


def _weight_fetch(wgu_hbm, wdn_hbm, gu_buf, dn_buf, sems, expert, slot):
    return (pltpu.make_async_copy(wgu_hbm.at[expert], gu_buf.at[slot], sems.at[slot, 0]),
            pltpu.make_async_copy(wdn_hbm.at[expert], dn_buf.at[slot], sems.at[slot, 1]))


def _experts_kernel(ce_ref, first_ref, next_ref, slot_ref, na_ref, x_ref, wgu_hbm, wdn_hbm, o_ref,
                    gu_buf, dn_buf, gu_bf, dn_bf, sems):
    ch = EXPERT_CHUNK
    nsl = wgu_hbm.shape[1] // LANES
    chunk_rows = ch * nsl
    fetch = functools.partial(_weight_fetch, wgu_hbm, wdn_hbm, gu_buf, dn_buf, sems)

    @pl.when(pl.program_id(0) == 0)
    def _():
        for cp in fetch(ce_ref[0], 0):
            cp.start()

    for sub in range(CHUNKS_PER_STEP):
        i = pl.program_id(0) * CHUNKS_PER_STEP + sub
        active = i < na_ref[0]
        base = sub * chunk_rows

        @pl.when(active & (first_ref[i] == 1))
        def _():
            slot = slot_ref[i]
            for cp in fetch(ce_ref[i], slot):
                cp.wait()

            @pl.when(next_ref[i] >= 0)
            def _():
                for cp in fetch(next_ref[i], 1 - slot):
                    cp.start()

            gu_bf[...] = gu_buf[slot].astype(BF16)
            dn_bf[...] = dn_buf[slot].astype(BF16)

        @pl.when(active)
        def _():
            x = jnp.concatenate(
                [x_ref[pl.ds(base + s, ch, stride=SUBLANES), :] for s in range(nsl)],
                axis=1).astype(BF16)
            au = _dot(x, gu_bf[...])
            a = au[:, :EXPERT_DIM]
            u = au[:, EXPERT_DIM:]
            hmid = (a * jax.nn.sigmoid(a) * u).astype(BF16)
            y = _dot(hmid, dn_bf[...])
            for s in range(nsl):
                o_ref[pl.ds(base + s, ch, stride=SUBLANES), :] = y[:, s * LANES:(s + 1) * LANES]

        @pl.when(jnp.logical_not(active))
        def _():
            o_ref[pl.ds(base, chunk_rows), :] = jnp.zeros((chunk_rows, LANES), F32)


def _experts(chunk_e, first, next_e, slot, n_active, xs2d, w_gu, w_dn):
    n_chunks = chunk_e.shape[0]
    _, d, two_e = w_gu.shape
    blk_rows = CHUNKS_PER_STEP * EXPERT_CHUNK * d // LANES
    assert n_chunks % CHUNKS_PER_STEP == 0

    def x_map(i, ce, fs, nx, sl, na):
        return (jnp.minimum(i, (na[0] - 1) // CHUNKS_PER_STEP), 0)

    grid_spec = pltpu.PrefetchScalarGridSpec(
        num_scalar_prefetch=5,
        grid=(n_chunks // CHUNKS_PER_STEP,),
        in_specs=[pl.BlockSpec((blk_rows, LANES), x_map),
                  pl.BlockSpec(memory_space=pl.ANY),
                  pl.BlockSpec(memory_space=pl.ANY)],
        out_specs=pl.BlockSpec((blk_rows, LANES), lambda i, ce, fs, nx, sl, na: (i, 0)),
        scratch_shapes=[pltpu.VMEM((2, d, two_e), F32), pltpu.VMEM((2, two_e // 2, d), F32),
                        pltpu.VMEM((d, two_e), BF16), pltpu.VMEM((two_e // 2, d), BF16),
                        pltpu.SemaphoreType.DMA((2, 2))],
    )
    return pl.pallas_call(
        _experts_kernel,
        grid_spec=grid_spec,
        out_shape=jax.ShapeDtypeStruct(xs2d.shape, F32),
        compiler_params=_params("arbitrary"),
    )(chunk_e, first, next_e, slot, n_active, xs2d, w_gu, w_dn)


def _combine_kernel(dest_ref, dest_next_ref, w_ref, y_ref, o_ref, buf, sem, *, tj):
    step = pl.program_id(0)
    n_steps = pl.num_programs(0)
    cur = step % 2

    def gather(dest, slot):
        def issue(tl, carry):
            for k in range(TOP_K):
                a = tl * TOP_K + k
                _row_copy(y_ref, dest[a], buf.at[slot], a, sem.at[slot]).start()
            return carry

        lax.fori_loop(0, tj, issue, 0)

    @pl.when(step == 0)
    def _():
        gather(dest_ref, 0)

    @pl.when(step + 1 < n_steps)
    def _():
        gather(dest_next_ref, 1 - cur)

    rows = buf.at[cur]

    def drain(tl, carry):
        for _ in range(TOP_K):
            _row_copy(y_ref, 0, rows, 0, sem.at[cur]).wait()
        return carry

    lax.fori_loop(0, tj, drain, 0)

    def reduce(tl, carry):
        acc = w_ref[tl * TOP_K] * rows[tl * TOP_K]
        for k in range(1, TOP_K):
            acc = acc + w_ref[tl * TOP_K + k] * rows[tl * TOP_K + k]
        o_ref[tl] = acc
        return carry

    lax.fori_loop(0, tj, reduce, 0)


def _combine(dest_flat, w_flat, y_rows, n_tok, tj):
    row_shape = y_rows.shape[1:]
    n_steps = n_tok // tj
    smem_blk = pl.BlockSpec((tj * TOP_K,), lambda i: (i,), memory_space=pltpu.SMEM)
    smem_next = pl.BlockSpec((tj * TOP_K,), lambda i: (jnp.minimum(i + 1, n_steps - 1),),
                             memory_space=pltpu.SMEM)
    return pl.pallas_call(
        functools.partial(_combine_kernel, tj=tj),
        grid=(n_steps,),
        in_specs=[smem_blk, smem_next, smem_blk,
                  pl.BlockSpec(memory_space=pl.ANY)],
        out_specs=pl.BlockSpec((tj,) + row_shape, lambda i: (i, 0, 0)),
        out_shape=jax.ShapeDtypeStruct((n_tok,) + row_shape, F32),
        scratch_shapes=[pltpu.VMEM((2, tj * TOP_K) + row_shape, F32),
                        pltpu.SemaphoreType.DMA((2,))],
        compiler_params=_params("arbitrary"),
    )(dest_flat, dest_flat, w_flat, y_rows)


def _final_kernel(hb_ref, r_ref, x1_ref, g2_ref, ng_ref, wgu_ref, wdn_ref, o_ref):
    tm, d = hb_ref.shape[1:]
    au = _dot(hb_ref[0], wgu_ref[...])
    a = au[:, :SHARED_DIM]
    u = au[:, SHARED_DIM:]
    shared = _dot((a * jax.nn.sigmoid(a) * u).astype(BF16), wdn_ref[...])
    routed = jnp.concatenate(
        [r_ref[pl.ds(s, tm, stride=SUBLANES), :] for s in range(d // LANES)], axis=1)
    o_ref[0] = x1_ref[0] + g2_ref[0] * _rms(routed + shared, ng_ref[...])


def _final(hb, routed2d, x1, g2, ng, wgu, wdn, tm):
    bsz, s, d = x1.shape
    nt = s // tm
    tile = pl.BlockSpec((1, tm, d), lambda bb, i: (bb, i, 0))
    return pl.pallas_call(
        _final_kernel,
        grid=(bsz, nt),
        in_specs=[tile,
                  pl.BlockSpec((tm * d // LANES, LANES), lambda bb, i: (bb * nt + i, 0)),
                  tile,
                  pl.BlockSpec((1, 1, d), lambda bb, i: (bb, 0, 0)),
                  pl.BlockSpec((1, d), lambda bb, i: (0, 0)),
                  pl.BlockSpec(wgu.shape, lambda bb, i: (0, 0)),
                  pl.BlockSpec(wdn.shape, lambda bb, i: (0, 0))],
        out_specs=tile,
        out_shape=jax.ShapeDtypeStruct((bsz, s, d), F32),
        compiler_params=_params("parallel", "parallel"),
    )(hb, routed2d, x1, g2, ng, wgu, wdn)


def _alibi_slopes(n):
    return (2.0 ** (-8.0 * jnp.arange(1, n + 1, dtype=F32) / n)).astype(F32)


def _layer(x, c, w_ada, b_ada, norm_g, w_in, w_cmp1, w_cmp2, pe_cmp, diff_lambda, diff_subln,
           w_out, w_router, router_bias, w_gate_up, w_down, ws_gate_up, ws_down, layer):
    bsz, s, d = x.shape
    n_tok = bsz * s
    tm = 256
    tq = 256
    tk = min(KEY_TILE, s // 2)
    tk_nsa = min(NSA_KEY_TILE, s // 2)
    dk = HEAD_DIM
    g = NSA_KV_HEADS

    mod = _ada(c, w_ada, b_ada)
    sh1, sc1, g1, sh2, sc2, g2 = [m[:, None, :] for m in jnp.split(mod, 6, axis=-1)]

    n_pad = _round_up(IN_WIDTH, LANES)
    w_in_p = jnp.zeros((d, n_pad), BF16).at[:, :IN_WIDTH].set(w_in.astype(BF16))
    proj = _inproj(x, sc1, sh1, norm_g[0:1], w_in_p, tm)

    offs = [0]
    for wdt in IN_SPLITS:
        offs.append(offs[-1] + wdt)
    pieces = [proj[..., offs[j]:offs[j + 1]] for j in range(len(IN_SPLITS))]
    nq, kcm, vcm, ksel, vsel, kwin, vwin, ngate, dq, dkk, dv = pieces
    scale = dk ** -0.5

    def heads_first(a, nh):
        return a.reshape(bsz, s, nh, dk).transpose(0, 2, 1, 3)

    def aug_q(qh, slopes):
        shape = [1] * qh.ndim
        shape[1] = slopes.shape[0]
        lane = jnp.arange(AUG_W - dk)
        feat = jnp.where(lane == 0, POS_SPLIT, jnp.where(lane == 1, 1.0, 0.0)) * slopes[:, None]
        feat = feat.astype(BF16).reshape(shape[:-1] + [AUG_W - dk])
        return jnp.concatenate([qh, jnp.broadcast_to(feat, qh.shape[:-1] + (AUG_W - dk,))], axis=-1)

    def aug_k(kh, block_onehot, tile):
        col = (jnp.arange(s, dtype=jnp.int32) % tile)[:, None]
        lane = jnp.arange(AUG_W - dk)[None, :]
        feat = jnp.where(lane == 0, col // POS_SPLIT, jnp.where(lane == 1, col % POS_SPLIT, 0))
        parts = [kh, jnp.broadcast_to(feat.astype(BF16), kh.shape[:-1] + (AUG_W - dk,))]
        if block_onehot:
            oh = (jnp.arange(s)[:, None] // SEL_BLOCK == jnp.arange(LANES)[None, :]).astype(BF16)
            parts.append(jnp.broadcast_to(oh, kh.shape[:-1] + (LANES,)))
        return jnp.concatenate(parts, axis=-1)

    q_nsa = aug_q(heads_first(nq * scale, NSA_HEADS), _alibi_slopes(NSA_HEADS))
    ks = aug_k(heads_first(ksel, g), True, tk_nsa)
    vs, kw, vw = [heads_first(a, g) for a in (vsel, kwin, vwin)]
    vs = jnp.concatenate([vs, jnp.ones(vs.shape[:-1] + (AUG_W - dk,), BF16)], axis=-1)
    gates = ngate.reshape(bsz, s, g, 3 * NSA_REP).transpose(0, 2, 1, 3)

    n_sub = CMP_BLOCK // CMP_STRIDE
    assert n_sub == 2
    ncp = s // CMP_STRIDE
    chunks = jnp.stack([heads_first(kcm, g), heads_first(vcm, g)]).reshape(
        2, bsz, g, ncp, CMP_STRIDE * dk)
    w1 = w_cmp1.astype(BF16).reshape(2, n_sub, CMP_STRIDE * dk, dk)
    pe8 = jnp.broadcast_to(pe_cmp.astype(BF16).reshape(2, 1, CMP_BLOCK * dk),
                           (2, SUBLANES, CMP_BLOCK * dk))
    cmp_kv = _compress(chunks, w1, w_cmp2.astype(BF16), pe8)

    n_blk = s // SEL_BLOCK
    assert n_blk <= LANES and min(SEL_TOPK, n_blk) == SEL_TOPK
    cs = jnp.arange(ncp)[:, None] * CMP_STRIDE
    bs = jnp.arange(LANES)[None, :] * SEL_BLOCK
    ov = (jnp.clip(jnp.minimum(cs + CMP_BLOCK, bs + SEL_BLOCK) - jnp.maximum(cs, bs), 0, None)
          .astype(F32) / CMP_BLOCK).astype(BF16)

    o_nsa = _nsa(_alibi_slopes(NSA_HEADS), q_nsa, cmp_kv[0], cmp_kv[1], ks, vs, kw, vw,
                 gates, ov, tq, tk_nsa)
    o_nsa = o_nsa.transpose(0, 2, 1, 3).reshape(bsz, s, NSA_Q_W)

    def maps_first(a):
        return a.reshape(bsz, s, DIFF_HEADS, 2, dk).transpose(0, 2, 3, 1, 4)

    lam_init = 0.8 - 0.6 * math.exp(-0.3 * layer)
    o_diff = _diff(_alibi_slopes(DIFF_HEADS),
                   aug_q(maps_first(dq * scale), _alibi_slopes(DIFF_HEADS)),
                   aug_k(maps_first(dkk), False, tk),
                   dv.reshape(bsz, s, DIFF_HEADS, DIFF_V_DIM).transpose(0, 2, 1, 3),
                   diff_lambda, diff_subln, min(DIFF_Q_TILE, tk), tk, lam_init)

    x1, hb, h_rows2d, logits = _outproj(o_nsa, o_diff, w_out.astype(BF16), x, g1, sc2, sh2,
                                        norm_g[1:2], norm_g[2:3], w_router, tm)

    oi, ow, cnt = _router(logits.reshape(n_tok, N_EXPERTS), router_bias, tm)
    idx_flat = oi[:, :TOP_K].reshape(-1)
    rank_flat = oi[:, TOP_K:2 * TOP_K].reshape(-1)
    w_flat = ow[:, :TOP_K].reshape(-1)
    counts = cnt[0].astype(jnp.int32)
    padded = (counts + EXPERT_CHUNK - 1) // EXPERT_CHUNK * EXPERT_CHUNK
    pend = jnp.cumsum(padded)
    pstart = (pend - padded).astype(jnp.int32)
    n_asg = n_tok * TOP_K
    n_chunks = (n_asg + N_EXPERTS * (EXPERT_CHUNK - 1) + EXPERT_CHUNK - 1) // EXPERT_CHUNK
    n_chunks = _round_up(n_chunks, CHUNKS_PER_STEP)
    n_rows = n_chunks * EXPERT_CHUNK
    chunk_start = jnp.arange(n_chunks, dtype=jnp.int32) * EXPERT_CHUNK
    chunk_e = jnp.minimum(
        jnp.sum((pend[None, :] <= chunk_start[:, None]).astype(jnp.int32), axis=1), N_EXPERTS - 1)
    n_active = (pend[-1:] // EXPERT_CHUNK).astype(jnp.int32)
    first = jnp.concatenate([jnp.ones((1,), jnp.int32),
                             (chunk_e[1:] != chunk_e[:-1]).astype(jnp.int32)])
    slot = (jnp.cumsum(first) - 1) % 2
    e_ids = jnp.arange(N_EXPERTS, dtype=jnp.int32)
    from_here = lax.cummin(jnp.where(counts > 0, e_ids, N_EXPERTS), axis=0, reverse=True)
    after = jnp.concatenate([from_here[1:], jnp.full((1,), N_EXPERTS, jnp.int32)])
    next_e = after[chunk_e]
    next_e = jnp.where(next_e == N_EXPERTS, -1, next_e).astype(jnp.int32)

    row_tile = (d // LANES, LANES)
    dest_flat = rank_flat + jnp.sum(
        jnp.where(idx_flat[:, None] == e_ids[None, :], pstart[None, :], 0), axis=1)
    xs = _dispatch(dest_flat, pstart, pend.astype(jnp.int32),
                   h_rows2d.reshape((n_tok,) + row_tile), n_rows, 128)
    ys = _experts(chunk_e, first, next_e, slot.astype(jnp.int32), n_active,
                  xs.reshape(n_rows * d // LANES, LANES), w_gate_up, w_down)
    routed = _combine(dest_flat, w_flat, ys.reshape((n_rows,) + row_tile), n_tok, 128)
    return _final(hb, routed.reshape(n_tok * d // LANES, LANES), x1, g2, norm_g[3:4],
                  ws_gate_up.astype(BF16), ws_down.astype(BF16), tm)


def kernel(x, c, w_ada, b_ada, norm_g, w_in, w_cmp1, w_cmp2, pe_cmp, diff_lambda, diff_subln,
           w_out, w_router, router_bias, w_gate_up, w_down, ws_gate_up, ws_down):
    for layer in range(w_ada.shape[0]):
        x = _layer(x, c, w_ada[layer], b_ada[layer], norm_g[layer], w_in[layer], w_cmp1[layer],
                   w_cmp2[layer], pe_cmp[layer], diff_lambda[layer], diff_subln[layer],
                   w_out[layer], w_router[layer], router_bias[layer], w_gate_up[layer],
                   w_down[layer], ws_gate_up[layer], ws_down[layer], layer)
    return x
```
